```python
import jax, jax.numpy as jnp
from jax import lax
import numpy as np

D_MODEL = 1024
BATCH = 4
SEQ = 8192
DEPTH = 2

GRID_W = 64
CTX_LEN = 256
EPS = 1e-6
ROPE_BASE = 10000.0
N_MOD = 9

D_FF = 2816

RET_HEADS = 8
RET_QK_DIM = 64
RET_V_DIM = 128
RET_CHUNK = 128
RET_QK_WIDTH = RET_HEADS * RET_QK_DIM
RET_V_WIDTH = RET_HEADS * RET_V_DIM
RET_SCALE = RET_QK_DIM ** -0.5

MLA_HEADS = 8
MLA_Q_RANK = 384
MLA_KV_RANK = 256
MLA_NOPE_DIM = 64
MLA_ROPE_DIM = 32
MLA_V_DIM = 64
MLA_QK_DIM = MLA_NOPE_DIM + MLA_ROPE_DIM
MLA_OUT_WIDTH = MLA_HEADS * MLA_V_DIM
MLA_SCALE = MLA_QK_DIM ** -0.5
Q_BLOCK = 128

IN_PARTS = (
    ("ret_q", RET_QK_WIDTH),
    ("ret_k", RET_QK_WIDTH),
    ("ret_v", RET_V_WIDTH),
    ("ret_g", RET_V_WIDTH),
    ("mla_dq", MLA_Q_RANK),
    ("mla_dkv", MLA_KV_RANK),
    ("mla_kr", MLA_ROPE_DIM),
    ("gate_ret", D_MODEL),
    ("gate_mla", D_MODEL),
)
IN_WIDTH = 2 * RET_QK_WIDTH + 2 * RET_V_WIDTH + MLA_Q_RANK + MLA_KV_RANK + MLA_ROPE_DIM + 2 * D_MODEL
ALL_PARTS = ("ret_q", "ret_k", "ret_v", "ret_g", "mla_dq", "mla_dkv", "mla_kr", "gate_ret", "gate_mla")
CTX_KV_PARTS = ("ret_k", "ret_v", "mla_dkv", "mla_kr")

kernel_name = "hybrid_retention_mla_macaron_dit"


def rmsnorm(x, gain=None):
    xf = x.astype(jnp.float32)
    y = xf * lax.rsqrt(jnp.mean(xf * xf, axis=-1, keepdims=True) + EPS)
    if gain is not None:
        y = y * gain.astype(jnp.float32)
    return y.astype(x.dtype)


def modulate(h, shift, scale):
    return h * (1.0 + scale) + shift


def swiglu(h, w1, w3, w2):
    return (jax.nn.silu(h @ w1) * (h @ w3)) @ w2


def heads(t, dim):
    return t.reshape(t.shape[0], t.shape[1], -1, dim)


def flip(t):
    return jnp.flip(t, axis=1)


def axial_rope_tables(length, n_freq):
    rows = length // GRID_W
    row = jnp.repeat(jnp.arange(rows, dtype=jnp.float32), GRID_W)
    col = jnp.tile(jnp.arange(GRID_W, dtype=jnp.float32), rows)
    inv_freq = jnp.power(ROPE_BASE, -jnp.arange(n_freq, dtype=jnp.float32) / n_freq)
    ang_r = row[:, None] * inv_freq[None, :]
    ang_c = col[:, None] * inv_freq[None, :]
    return (jnp.cos(ang_r), jnp.sin(ang_r), jnp.cos(ang_c), jnp.sin(ang_c))


def _rotate(x, cos, sin):
    n = cos.shape[-1]
    x1, x2 = x[..., :n], x[..., n:]
    cos = cos[:, None, :]
    sin = sin[:, None, :]
    return jnp.concatenate([x1 * cos - x2 * sin, x2 * cos + x1 * sin], axis=-1)


def axial_rope(x, tables):
    cos_r, sin_r, cos_c, sin_c = tables
    half = x.shape[-1] // 2
    xf = x.astype(jnp.float32)
    out = jnp.concatenate([_rotate(xf[..., :half], cos_r, sin_r),
                           _rotate(xf[..., half:], cos_c, sin_c)], axis=-1)
    return out.astype(x.dtype)


def in_projection(h, w_in, names):
    offsets = {}
    start = 0
    for name, width in IN_PARTS:
        offsets[name] = (start, width)
        start += width
    w = jnp.concatenate([w_in[:, offsets[n][0]:offsets[n][0] + offsets[n][1]] for n in names], axis=1)
    y = h @ w
    out = {}
    pos = 0
    for n in names:
        width = offsets[n][1]
        out[n] = y[..., pos:pos + width]
        pos += width
    return out


def retention_chunkwise(q, k, v, log_g, state0):
    B, L, H, dk = q.shape
    dv = v.shape[-1]
    C = RET_CHUNK
    N = L // C
    q = q.astype(jnp.float32).reshape(B, N, C, H, dk)
    k = k.astype(jnp.float32).reshape(B, N, C, H, dk)
    v = v.astype(jnp.float32).reshape(B, N, C, H, dv)
    idx = jnp.arange(C, dtype=jnp.float32)
    diff = idx[:, None] - idx[None, :]
    decay_intra = jnp.where(diff >= 0, jnp.exp(log_g[:, None, None] * jnp.maximum(diff, 0.0)), 0.0)
    scores = jnp.einsum('bnihd,bnjhd->bnhij', q, k) * decay_intra
    o_intra = jnp.einsum('bnhij,bnjhe->bnihe', scores, v)
    k_dec = k * jnp.exp(log_g[None, :] * (C - 1.0 - idx)[:, None])[:, :, None]
    kv_chunk = jnp.einsum('bnjhd,bnjhe->nbhde', k_dec, v)
    chunk_decay = jnp.exp(log_g * C)[None, :, None, None]

    def step(state, kv_n):
        return chunk_decay * state + kv_n, state

    final_state, state_prev = lax.scan(step, state0, kv_chunk)
    q_dec = q * jnp.exp(log_g[None, :] * (idx + 1.0)[:, None])[:, :, None]
    o_cross = jnp.einsum('bnihd,nbhde->bnihe', q_dec, state_prev)
    return (o_intra + o_cross).reshape(B, L, H, dv), final_state


def decayed_state(k, v, log_g):
    L = k.shape[1]
    pos = jnp.arange(L, dtype=jnp.float32)
    w = jnp.exp(log_g[None, :] * (L - 1.0 - pos)[:, None])
    return jnp.einsum('blhd,lh,blhe->bhde', k.astype(jnp.float32), w, v.astype(jnp.float32))


def retention_output(o, g_raw, ret_gn, w_ret_out):
    B, L = o.shape[0], o.shape[1]
    mu = jnp.mean(o, axis=-1, keepdims=True)
    var = jnp.mean(jnp.square(o - mu), axis=-1, keepdims=True)
    n = ((o - mu) * lax.rsqrt(var + EPS)).reshape(B, L, RET_V_WIDTH) * ret_gn.astype(jnp.float32)
    return (jax.nn.silu(g_raw) * n.astype(g_raw.dtype)) @ w_ret_out


def ret_qk(t, rope):
    t = heads(t, RET_QK_DIM)
    return t if rope is None else axial_rope(t, rope)


def mla_queries(dq, q_norm, w_uq, rope):
    q = heads(rmsnorm(dq, q_norm) @ w_uq, MLA_QK_DIM)
    if rope is None:
        return q
    return jnp.concatenate([q[..., :MLA_NOPE_DIM], axial_rope(q[..., MLA_NOPE_DIM:], rope)], axis=-1)


def mla_keys_values(dkv, kr, kv_norm, w_ukv, rope):
    kv = heads(rmsnorm(dkv, kv_norm) @ w_ukv, MLA_NOPE_DIM + MLA_V_DIM)
    k_nope, v = kv[..., :MLA_NOPE_DIM], kv[..., MLA_NOPE_DIM:]
    k_rope = kr[:, :, None, :]
    if rope is not None:
        k_rope = axial_rope(k_rope, rope)
    k_rope = jnp.broadcast_to(k_rope, k_nope.shape[:-1] + (MLA_ROPE_DIM,))
    return jnp.concatenate([k_nope, k_rope], axis=-1), v


def softmax_attend(q, k, v):
    s = jnp.einsum('bqhd,bkhd->bhqk', q, k).astype(jnp.float32) * MLA_SCALE
    p = jax.nn.softmax(s, axis=-1).astype(v.dtype)
    return jnp.einsum('bhqk,bkhe->bqhe', p, v)


def blocked_attend(q, k, v):
    B, L, H, d = q.shape
    qb = q.reshape(B, L // Q_BLOCK, Q_BLOCK, H, d).transpose(1, 0, 2, 3, 4)
    o = lax.map(lambda blk: softmax_attend(blk, k, v), qb)
    return o.transpose(1, 0, 2, 3, 4).reshape(B, L, H, v.shape[-1])


def merge(ret_b, mla_b, gate_ret, gate_mla, w_o):
    return (jax.nn.sigmoid(gate_ret) * ret_b + jax.nn.sigmoid(gate_mla) * mla_b) @ w_o


def token_mixer(h_lat, h_ctx, w_in, ret_decay_fwd, ret_decay_bwd, ret_gn,
                mla_q_norm, mla_kv_norm, w_uq, w_ukv, w_ret_out, w_mla_out, w_o,
                ret_rope, mla_rope, need_ctx_out):
    B, L = h_lat.shape[0], h_lat.shape[1]
    log_g_f = -jnp.exp(ret_decay_fwd.astype(jnp.float32))
    log_g_b = -jnp.exp(ret_decay_bwd.astype(jnp.float32))

    pl = in_projection(h_lat, w_in, ALL_PARTS)
    pc = in_projection(h_ctx, w_in, ALL_PARTS if need_ctx_out else CTX_KV_PARTS)

    kc = ret_qk(pc["ret_k"], None) * RET_SCALE
    vc = heads(pc["ret_v"], RET_V_DIM)
    if need_ctx_out:
        zero_state = jnp.zeros((B, RET_HEADS, RET_QK_DIM, RET_V_DIM), jnp.float32)
        qc = ret_qk(pc["ret_q"], None)
        oc_f, sc_f = retention_chunkwise(qc, kc, vc, log_g_f, zero_state)
        oc_b, sc_b = retention_chunkwise(flip(qc), flip(kc), flip(vc), log_g_b, zero_state)
        ret_c = retention_output(oc_f + flip(oc_b), pc["ret_g"], ret_gn, w_ret_out)
    else:
        sc_f = decayed_state(kc, vc, log_g_f)
        sc_b = decayed_state(flip(kc), flip(vc), log_g_b)

    ql = ret_qk(pl["ret_q"], ret_rope)
    kl = ret_qk(pl["ret_k"], ret_rope) * RET_SCALE
    vl = heads(pl["ret_v"], RET_V_DIM)
    ol_f, _ = retention_chunkwise(ql, kl, vl, log_g_f, sc_f)
    ol_b, _ = retention_chunkwise(flip(ql), flip(kl), flip(vl), log_g_b, sc_b)
    ret_l = retention_output(ol_f + flip(ol_b), pl["ret_g"], ret_gn, w_ret_out)

    kc_m, vc_m = mla_keys_values(pc["mla_dkv"], pc["mla_kr"], mla_kv_norm, w_ukv, None)
    ql_m = mla_queries(pl["mla_dq"], mla_q_norm, w_uq, mla_rope)
    kl_m, vl_m = mla_keys_values(pl["mla_dkv"], pl["mla_kr"], mla_kv_norm, w_ukv, mla_rope)
    k_all = jnp.concatenate([kl_m, kc_m], axis=1)
    v_all = jnp.concatenate([vl_m, vc_m], axis=1)
    mla_l = blocked_attend(ql_m, k_all, v_all).reshape(B, L, MLA_OUT_WIDTH) @ w_mla_out
    out_lat = merge(ret_l, mla_l, pl["gate_ret"], pl["gate_mla"], w_o)

    if not need_ctx_out:
        return out_lat, None
    qc_m = mla_queries(pc["mla_dq"], mla_q_norm, w_uq, None)
    mla_c = softmax_attend(qc_m, kc_m, vc_m).reshape(B, h_ctx.shape[1], MLA_OUT_WIDTH) @ w_mla_out
    out_ctx = merge(ret_c, mla_c, pc["gate_ret"], pc["gate_mla"], w_o)
    return out_lat, out_ctx


def setup_inputs(seed: int = 0) -> dict:
    key = jax.random.key(seed)
    ks = jax.random.split(key, 32)
    f32 = jnp.float32

    def nrm(k, shape, scale):
        return jax.random.normal(k, shape, f32) * scale

    base_decay = jnp.log(-jnp.log1p(-jnp.power(2.0, -5.0 - jnp.arange(RET_HEADS, dtype=f32))))
    return {
        "x": nrm(ks[0], (BATCH, SEQ, D_MODEL), 1.0),
        "c": nrm(ks[1], (BATCH, D_MODEL), 1.0),
        "ctx": nrm(ks[2], (BATCH, CTX_LEN, D_MODEL), 1.0),
        "c_ctx": nrm(ks[3], (D_MODEL,), 1.0),
        "w_ada": nrm(ks[4], (DEPTH, D_MODEL, N_MOD * D_MODEL), 0.5 * D_MODEL ** -0.5),
        "b_ada": nrm(ks[5], (DEPTH, N_MOD * D_MODEL), 0.02),
        "ffn1_w1": nrm(ks[6], (DEPTH, D_MODEL, D_FF), D_MODEL ** -0.5),
        "ffn1_w3": nrm(ks[7], (DEPTH, D_MODEL, D_FF), D_MODEL ** -0.5),
        "ffn1_w2": nrm(ks[8], (DEPTH, D_FF, D_MODEL), D_FF ** -0.5),
        "ffn2_w1": nrm(ks[9], (DEPTH, D_MODEL, D_FF), D_MODEL ** -0.5),
        "ffn2_w3": nrm(ks[10], (DEPTH, D_MODEL, D_FF), D_MODEL ** -0.5),
        "ffn2_w2": nrm(ks[11], (DEPTH, D_FF, D_MODEL), D_FF ** -0.5),
        "w_in": nrm(ks[12], (DEPTH, D_MODEL, IN_WIDTH), D_MODEL ** -0.5),
        "ret_decay_fwd": base_decay[None, :] + nrm(ks[13], (DEPTH, RET_HEADS), 0.05),
        "ret_decay_bwd": base_decay[None, :] + nrm(ks[14], (DEPTH, RET_HEADS), 0.05),
        "ret_gn": 1.0 + nrm(ks[15], (DEPTH, RET_V_WIDTH), 0.05),
        "mla_q_norm": 1.0 + nrm(ks[16], (DEPTH, MLA_Q_RANK), 0.05),
        "mla_kv_norm": 1.0 + nrm(ks[17], (DEPTH, MLA_KV_RANK), 0.05),
        "w_uq": nrm(ks[18], (DEPTH, MLA_Q_RANK, MLA_HEADS * MLA_QK_DIM), MLA_Q_RANK ** -0.5),
        "w_ukv": nrm(ks[19], (DEPTH, MLA_KV_RANK, MLA_HEADS * (MLA_NOPE_DIM + MLA_V_DIM)), MLA_KV_RANK ** -0.5),
        "w_ret_out": nrm(ks[20], (DEPTH, RET_V_WIDTH, D_MODEL), RET_V_WIDTH ** -0.5),
        "w_mla_out": nrm(ks[21], (DEPTH, MLA_OUT_WIDTH, D_MODEL), MLA_OUT_WIDTH ** -0.5),
        "w_o": nrm(ks[22], (DEPTH, D_MODEL, D_MODEL), D_MODEL ** -0.5),
        "final_norm": 1.0 + nrm(ks[23], (D_MODEL,), 0.05),
    }


def reference(x, c, ctx, c_ctx, w_ada, b_ada, ffn1_w1, ffn1_w3, ffn1_w2,
              ffn2_w1, ffn2_w3, ffn2_w2, w_in, ret_decay_fwd, ret_decay_bwd, ret_gn,
              mla_q_norm, mla_kv_norm, w_uq, w_ukv, w_ret_out, w_mla_out, w_o, final_norm):
    seq_len = x.shape[1]
    ret_rope = axial_rope_tables(seq_len, RET_QK_DIM // 4)
    mla_rope = axial_rope_tables(seq_len, MLA_ROPE_DIM // 4)
    silu_c = jax.nn.silu(c)
    silu_cc = jax.nn.silu(c_ctx)
    xc = ctx
    for l in range(DEPTH):
        last = l == DEPTH - 1
        mod = (silu_c @ w_ada[l] + b_ada[l])[:, None, :]
        mod_c = (silu_cc @ w_ada[l] + b_ada[l])[None, None, :]
        sh1, sc1, g1, sh2, sc2, g2, sh3, sc3, g3 = jnp.split(mod, N_MOD, axis=-1)
        csh1, csc1, cg1, csh2, csc2, cg2, csh3, csc3, cg3 = jnp.split(mod_c, N_MOD, axis=-1)

        x = x + 0.5 * g1 * swiglu(modulate(rmsnorm(x), sh1, sc1), ffn1_w1[l], ffn1_w3[l], ffn1_w2[l])
        xc = xc + 0.5 * cg1 * swiglu(modulate(rmsnorm(xc), csh1, csc1), ffn1_w1[l], ffn1_w3[l], ffn1_w2[l])

        o_lat, o_ctx = token_mixer(
            modulate(rmsnorm(x), sh2, sc2), modulate(rmsnorm(xc), csh2, csc2),
            w_in[l], ret_decay_fwd[l], ret_decay_bwd[l], ret_gn[l],
            mla_q_norm[l], mla_kv_norm[l], w_uq[l], w_ukv[l],
            w_ret_out[l], w_mla_out[l], w_o[l], ret_rope, mla_rope, not last)
        x = x + g2 * o_lat

        x = x + 0.5 * g3 * swiglu(modulate(rmsnorm(x), sh3, sc3), ffn2_w1[l], ffn2_w3[l], ffn2_w2[l])
        if not last:
            xc = xc + cg2 * o_ctx
            xc = xc + 0.5 * cg3 * swiglu(modulate(rmsnorm(xc), csh3, csc3), ffn2_w1[l], ffn2_w3[l], ffn2_w2[l])
    return rmsnorm(x, final_norm)
```

```python
import functools

import jax
import jax.numpy as jnp
from jax import lax
from jax.experimental import pallas as pl
from jax.experimental.pallas import tpu as pltpu

F32 = jnp.float32
BF16 = jnp.bfloat16

GRID_W = 64
EPS = 1e-6
ROPE_BASE = 10000.0
N_MOD = 9

RET_HEADS = 8
RET_QK_DIM = 64
RET_V_DIM = 128
RET_CHUNK = 128
RET_QK_WIDTH = RET_HEADS * RET_QK_DIM
RET_V_WIDTH = RET_HEADS * RET_V_DIM
RET_SCALE = RET_QK_DIM ** -0.5

MLA_HEADS = 8
MLA_Q_RANK = 384
MLA_KV_RANK = 256
MLA_NOPE_DIM = 64
MLA_ROPE_DIM = 32
MLA_V_DIM = 64
MLA_QK_DIM = MLA_NOPE_DIM + MLA_ROPE_DIM
MLA_OUT_WIDTH = MLA_HEADS * MLA_V_DIM
MLA_SCALE = MLA_QK_DIM ** -0.5

LANES = 128
MLA_HEAD_PAD = LANES
VMEM_LIMIT = 56 * 1024 * 1024

_O_RQ = 0
_O_RK = _O_RQ + RET_QK_WIDTH
_O_RV = _O_RK + RET_QK_WIDTH
_O_RG = _O_RV + RET_V_WIDTH
_O_DQ = _O_RG + RET_V_WIDTH
_O_DKV = _O_DQ + MLA_Q_RANK
_O_KR = _O_DKV + MLA_KV_RANK
_O_GR = _O_KR + LANES
_O_GM = _O_GR + RET_V_WIDTH
_IN_W = _O_GM + RET_V_WIDTH


def _cparams(sem, vmem=VMEM_LIMIT):
    return pltpu.CompilerParams(dimension_semantics=sem, vmem_limit_bytes=vmem)


def _const_spec(shape):
    n = len(shape)
    return pl.BlockSpec(shape, lambda *_: (0,) * n, pipeline_mode=pl.Buffered(1))


def _rms(xf):
    return xf * lax.rsqrt(jnp.mean(xf * xf, axis=-1, keepdims=True) + EPS)


def _silu(a):
    return a * jax.nn.sigmoid(a)


def _mod_kernel(c_ref, w_ref, b_ref, o_ref):
    s = _silu(c_ref[...])
    o_ref[0] = jnp.dot(s, w_ref[0], preferred_element_type=F32,
                       precision=lax.Precision.HIGHEST) + b_ref[0]


def _mod_call(cc, w_ada, b_ada):
    depth, d, nd = w_ada.shape
    rows = cc.shape[0]
    tn = 1024
    return pl.pallas_call(
        _mod_kernel,
        grid=(depth, nd // tn),
        in_specs=[
            pl.BlockSpec((rows, d), lambda l, j: (0, 0)),
            pl.BlockSpec((1, d, tn), lambda l, j: (l, 0, j)),
            pl.BlockSpec((1, 1, tn), lambda l, j: (l, 0, j)),
        ],
        out_specs=pl.BlockSpec((1, rows, tn), lambda l, j: (l, 0, j)),
        out_shape=jax.ShapeDtypeStruct((depth, rows, nd), F32),
        compiler_params=_cparams(("arbitrary", "arbitrary")),
        name="mod",
    )(cc, w_ada, b_ada.reshape(depth, 1, nd))


def _ffn_kernel(x_ref, sh_ref, sc_ref, g_ref, w1_ref, w3_ref, w2_ref, *rest, final):
    if final:
        fn_ref, o_ref = rest
    else:
        (o_ref,) = rest
    xf = x_ref[0]
    h = _rms(xf) * (1.0 + sc_ref[0]) + sh_ref[0]
    hb = h.astype(BF16)
    a = jnp.dot(hb, w1_ref[...], preferred_element_type=F32)
    b = jnp.dot(hb, w3_ref[...], preferred_element_type=F32)
    u = (_silu(a) * b).astype(BF16)
    y = jnp.dot(u, w2_ref[...], preferred_element_type=F32)
    xn = xf + (0.5 * g_ref[0]) * y
    if final:
        xn = _rms(xn) * fn_ref[...]
    o_ref[0] = xn


def _mod_spec(d, k):
    return pl.BlockSpec((1, 1, d), lambda b, i: (b, 0, k))


def _ffn_call(x, mod, k0, w1, w3, w2, final_gain=None):
    bsz, seq, d = x.shape
    dff = w1.shape[1]
    tm = min(512, seq)
    final = final_gain is not None
    in_specs = [
        pl.BlockSpec((1, tm, d), lambda b, i: (b, i, 0)),
        _mod_spec(d, k0), _mod_spec(d, k0 + 1), _mod_spec(d, k0 + 2),
        _const_spec((d, dff)), _const_spec((d, dff)), _const_spec((dff, d)),
    ]
    args = [x, mod, mod, mod, w1, w3, w2]
    if final:
        in_specs.append(_const_spec((1, d)))
        args.append(final_gain.reshape(1, d))
    return pl.pallas_call(
        functools.partial(_ffn_kernel, final=final),
        grid=(bsz, seq // tm),
        in_specs=in_specs,
        out_specs=pl.BlockSpec((1, tm, d), lambda b, i: (b, i, 0)),
        out_shape=jax.ShapeDtypeStruct(x.shape, F32),
        compiler_params=_cparams(("parallel", "parallel")),
        name="ffn_final" if final else "ffn",
    )(*args)


def _rope(y, cos, sin, first_half, off):
    partner = jnp.where(first_half, pltpu.roll(y, LANES - off, 1), pltpu.roll(y, off, 1))
    return y * cos + partner * sin


def _inproj_kernel(x_ref, sh_ref, sc_ref, w_ref, qn_ref, kvn_ref, wuq_ref, wukv_ref,
                   rcos_ref, rsin_ref, mcos_ref, msin_ref,
                   rq_ref, rk_ref, rv_ref, rg_ref, qm_ref, km_ref, vm_ref, gr_ref, gm_ref):
    tm = x_ref.shape[1]
    xf = x_ref[0]
    hb = (_rms(xf) * (1.0 + sc_ref[0]) + sh_ref[0]).astype(BF16)

    def proj(a, b):
        return jnp.dot(hb, w_ref[:, a:b], preferred_element_type=F32)

    lane = lax.broadcasted_iota(jnp.int32, (tm, LANES), 1)
    ret_first = (lane % (RET_QK_DIM // 2)) < (RET_QK_DIM // 4)
    mla_first = (lane % (MLA_ROPE_DIM // 2)) < (MLA_ROPE_DIM // 4)
    rcos, rsin = rcos_ref[...], rsin_ref[...]
    mcos, msin = mcos_ref[...], msin_ref[...]

    q = proj(_O_RQ, _O_RK)
    k = proj(_O_RK, _O_RV)
    for g in range(RET_QK_WIDTH // LANES):
        sl = slice(g * LANES, (g + 1) * LANES)
        rq_ref[0, :, sl] = _rope(q[:, sl], rcos, rsin, ret_first, RET_QK_DIM // 4).astype(BF16)
        rk_ref[0, :, sl] = (_rope(k[:, sl], rcos, rsin, ret_first, RET_QK_DIM // 4) * RET_SCALE).astype(BF16)
    rv_ref[0] = proj(_O_RV, _O_RG).astype(BF16)
    rg_ref[0] = proj(_O_RG, _O_DQ)

    dq = _rms(proj(_O_DQ, _O_DKV)) * qn_ref[...]
    qm = jnp.dot(dq.astype(BF16), wuq_ref[...], preferred_element_type=F32)
    dkv = _rms(proj(_O_DKV, _O_KR)) * kvn_ref[...]
    kv = jnp.dot(dkv.astype(BF16), wukv_ref[...], preferred_element_type=F32)
    kr = _rope(proj(_O_KR, _O_GR), mcos, msin, mla_first, MLA_ROPE_DIM // 4)
    for h in range(MLA_HEADS):
        sl = slice(h * LANES, (h + 1) * LANES)
        qh = _rope(qm[:, sl], mcos, msin, mla_first, MLA_ROPE_DIM // 4)
        qm_ref[0, :, sl] = (qh * MLA_SCALE).astype(BF16)
        km_ref[0, :, sl] = (kv[:, sl] + kr).astype(BF16)
    vm_ref[0] = kv[:, MLA_HEADS * LANES:].astype(BF16)

    gr_ref[0] = proj(_O_GR, _O_GM)
    gm_ref[0] = proj(_O_GM, _IN_W)


def _inproj_call(x, mod, k0, w, q_norm, kv_norm, wuq, wukv, tables):
    bsz, seq, d = x.shape
    tm = min(256, seq)
    row = lambda w_: pl.BlockSpec((1, tm, w_), lambda b, i: (b, i, 0))
    tab = pl.BlockSpec((tm, LANES), lambda b, i: (i, 0))
    widths = (RET_QK_WIDTH, RET_QK_WIDTH, RET_V_WIDTH, RET_V_WIDTH,
              MLA_HEADS * LANES, MLA_HEADS * LANES, MLA_OUT_WIDTH, RET_V_WIDTH, RET_V_WIDTH)
    dtypes = (BF16, BF16, BF16, F32, BF16, BF16, BF16, F32, F32)
    return pl.pallas_call(
        _inproj_kernel,
        grid=(bsz, seq // tm),
        in_specs=[
            row(d), _mod_spec(d, k0), _mod_spec(d, k0 + 1),
            _const_spec(w.shape), _const_spec((1, MLA_Q_RANK)), _const_spec((1, MLA_KV_RANK)),
            _const_spec(wuq.shape), _const_spec(wukv.shape),
            tab, tab, tab, tab,
        ],
        out_specs=[row(w_) for w_ in widths],
        out_shape=[jax.ShapeDtypeStruct((bsz, seq, w_), dt) for w_, dt in zip(widths, dtypes)],
        compiler_params=_cparams(("parallel", "parallel")),
        name="inproj",
    )(x, mod, mod, w, q_norm.reshape(1, -1), kv_norm.reshape(1, -1), wuq, wukv, *tables)


def _ret_kernel(qf_ref, kf_ref, vf_ref, qb_ref, kb_ref, vb_ref, decf_ref, decb_ref, sf0_ref, sb0_ref,
                of_ref, ob_ref, sf_ref, sb_ref,
                st_f, st_b, dm_f, dm_b, qs_f, qs_b, ks_f, ks_b, cd_f, cd_b):
    n = pl.program_id(1)
    c = qf_ref.shape[1]
    dk, dv = RET_QK_DIM, RET_V_DIM

    @pl.when(n == 0)
    def _():
        st_f[...] = sf0_ref[0]
        st_b[...] = sb0_ref[0]
        ii = lax.broadcasted_iota(jnp.int32, (c, c), 0).astype(F32)
        jj = lax.broadcasted_iota(jnp.int32, (c, c), 1).astype(F32)
        diff = ii - jj
        rv_ = lax.broadcasted_iota(jnp.int32, (c, dv), 0).astype(F32)
        rk_ = lax.broadcasted_iota(jnp.int32, (c, dk), 0).astype(F32)
        for h in range(RET_HEADS):
            lgf = -jnp.exp(decf_ref[h])
            lgb = -jnp.exp(decb_ref[h])
            dm_f[h] = jnp.where(diff >= 0, jnp.exp(lgf * jnp.maximum(diff, 0.0)), 0.0)
            dm_b[h] = jnp.where(diff <= 0, jnp.exp(lgb * jnp.maximum(-diff, 0.0)), 0.0)
            qs_f[h] = jnp.exp(lgf * (rv_ + 1.0))
            qs_b[h] = jnp.exp(lgb * (c - rv_))
            ks_f[h] = jnp.exp(lgf * (c - 1.0 - rk_))
            ks_b[h] = jnp.exp(lgb * rk_)
            cd_f[h] = jnp.broadcast_to(jnp.exp(lgf * c), (1, dv))
            cd_b[h] = jnp.broadcast_to(jnp.exp(lgb * c), (1, dv))

    def one_dir(q_ref, k_ref, v_ref, o_ref, st, dm, qs, ks, cd):
        for h in range(RET_HEADS):
            q = q_ref[0, :, h * dk:(h + 1) * dk]
            k = k_ref[0, :, h * dk:(h + 1) * dk]
            v = v_ref[0, :, h * dv:(h + 1) * dv]
            s = lax.dot_general(q, k, (((1,), (1,)), ((), ())), preferred_element_type=F32)
            p = (s * dm[h]).astype(BF16)
            state = st[h]
            o = jnp.dot(p, v, preferred_element_type=F32)
            o += qs[h] * jnp.dot(q, state.astype(BF16), preferred_element_type=F32)
            o_ref[0, :, h * dv:(h + 1) * dv] = o
            kd = (k.astype(F32) * ks[h]).astype(BF16)
            st[h] = cd[h] * state + lax.dot_general(
                kd, v, (((0,), (0,)), ((), ())), preferred_element_type=F32)

    one_dir(qf_ref, kf_ref, vf_ref, of_ref, st_f, dm_f, qs_f, ks_f, cd_f)
    one_dir(qb_ref, kb_ref, vb_ref, ob_ref, st_b, dm_b, qs_b, ks_b, cd_b)

    @pl.when(n == pl.num_programs(1) - 1)
    def _():
        sf_ref[0] = st_f[...]
        sb_ref[0] = st_b[...]


def _ret_call(rq, rk, rv, dec_f, dec_b, s0_f, s0_b):
    bsz, seq, _ = rq.shape
    c = RET_CHUNK
    nc = seq // c
    h, dk, dv = RET_HEADS, RET_QK_DIM, RET_V_DIM
    fwd = lambda w_: pl.BlockSpec((1, c, w_), lambda b, n: (b, n, 0))
    bwd = lambda w_: pl.BlockSpec((1, c, w_), lambda b, n: (b, nc - 1 - n, 0))
    st_spec = pl.BlockSpec((1, h, dk, dv), lambda b, n: (b, 0, 0, 0))
    dec_spec = pl.BlockSpec((h, 1, 1), lambda b, n: (0, 0, 0))
    return pl.pallas_call(
        _ret_kernel,
        grid=(bsz, nc),
        in_specs=[fwd(h * dk), fwd(h * dk), fwd(h * dv), bwd(h * dk), bwd(h * dk), bwd(h * dv),
                  dec_spec, dec_spec, st_spec, st_spec],
        out_specs=[fwd(h * dv), bwd(h * dv), st_spec, st_spec],
        out_shape=[jax.ShapeDtypeStruct((bsz, seq, h * dv), F32)] * 2
        + [jax.ShapeDtypeStruct((bsz, h, dk, dv), F32)] * 2,
        scratch_shapes=[
            pltpu.VMEM((h, dk, dv), F32), pltpu.VMEM((h, dk, dv), F32),
            pltpu.VMEM((h, c, c), F32), pltpu.VMEM((h, c, c), F32),
            pltpu.VMEM((h, c, dv), F32), pltpu.VMEM((h, c, dv), F32),
            pltpu.VMEM((h, c, dk), F32), pltpu.VMEM((h, c, dk), F32),
            pltpu.VMEM((h, 1, dv), F32), pltpu.VMEM((h, 1, dv), F32),
        ],
        compiler_params=_cparams(("parallel", "arbitrary")),
        name="retention",
    )(rq, rk, rv, rq, rk, rv, dec_f.reshape(h, 1, 1), dec_b.reshape(h, 1, 1), s0_f, s0_b)


def _attn_kernel(q_ref, k_ref, v_ref, o_ref, *, tk):
    tq = q_ref.shape[1]
    nk = k_ref.shape[1] // tk
    outs = []
    for hd in range(2):
        q = q_ref[0, :, hd * LANES:(hd + 1) * LANES]

        def body(j, carry, q=q, hd=hd):
            m, l, acc = carry
            ks = pl.multiple_of(j * tk, tk)
            k = k_ref[0, pl.ds(ks, tk), hd * LANES:(hd + 1) * LANES]
            v = v_ref[0, pl.ds(ks, tk), :]
            s = lax.dot_general(q, k, (((1,), (1,)), ((), ())), preferred_element_type=F32)
            m_new = jnp.maximum(m, jnp.max(s, axis=-1, keepdims=True))
            alpha = jnp.exp(m - m_new)
            p = jnp.exp(s - m_new)
            l = alpha * l + jnp.sum(p, axis=-1, keepdims=True)
            acc = alpha * acc + jnp.dot(p.astype(BF16), v, preferred_element_type=F32)
            return m_new, l, acc

        m0 = jnp.full((tq, 1), -jnp.inf, F32)
        l0 = jnp.zeros((tq, 1), F32)
        a0 = jnp.zeros((tq, LANES), F32)
        _, l, acc = lax.fori_loop(0, nk, body, (m0, l0, a0))
        outs.append(acc / l)
    lane = lax.broadcasted_iota(jnp.int32, (tq, LANES), 1)
    o_ref[0] = jnp.where(lane < MLA_V_DIM, outs[0], outs[1]).astype(o_ref.dtype)


def _attn_call(qm, km, vm):
    bsz, seq, _ = qm.shape
    lk = km.shape[1]
    tq = min(256, seq)
    tk = next(t for t in (768, 512, 256, 128) if lk % t == 0)
    return pl.pallas_call(
        functools.partial(_attn_kernel, tk=tk),
        grid=(bsz, MLA_HEADS // 2, seq // tq),
        in_specs=[
            pl.BlockSpec((1, tq, 2 * LANES), lambda b, h, i: (b, i, h)),
            pl.BlockSpec((1, lk, 2 * LANES), lambda b, h, i: (b, 0, h)),
            pl.BlockSpec((1, lk, LANES), lambda b, h, i: (b, 0, h)),
        ],
        out_specs=pl.BlockSpec((1, tq, LANES), lambda b, h, i: (b, i, h)),
        out_shape=jax.ShapeDtypeStruct((bsz, seq, MLA_OUT_WIDTH), BF16),
        compiler_params=_cparams(("parallel", "parallel", "parallel")),
        name="attention",
    )(qm, km, vm)


def _merge_kernel(x_ref, of_ref, ob_ref, rg_ref, at_ref, gr_ref, gm_ref, gn_ref, g2_ref,
                  wr_ref, wm_ref, wo_ref, o_ref, a_scr):
    dv = RET_V_DIM
    for h in range(RET_HEADS):
        sl = slice(h * dv, (h + 1) * dv)
        o = of_ref[0, :, sl] + ob_ref[0, :, sl]
        d = o - jnp.mean(o, axis=-1, keepdims=True)
        nrm = d * lax.rsqrt(jnp.mean(d * d, axis=-1, keepdims=True) + EPS) * gn_ref[:, sl]
        a_scr[:, sl] = (_silu(rg_ref[0, :, sl]) * nrm).astype(BF16)
    ret_l = jnp.dot(a_scr[...], wr_ref[...], preferred_element_type=F32)
    mla_l = jnp.dot(at_ref[0], wm_ref[...], preferred_element_type=F32)
    mrg = jax.nn.sigmoid(gr_ref[0]) * ret_l + jax.nn.sigmoid(gm_ref[0]) * mla_l
    out = jnp.dot(mrg.astype(BF16), wo_ref[...], preferred_element_type=F32)
    o_ref[0] = x_ref[0] + g2_ref[0] * out


def _merge_call(x, o_f, o_b, rg, att, gr, gm, gn, mod, k_gate, wr, wm, wo):
    bsz, seq, d = x.shape
    tm = min(256, seq)
    row = lambda w_: pl.BlockSpec((1, tm, w_), lambda b, i: (b, i, 0))
    return pl.pallas_call(
        _merge_kernel,
        grid=(bsz, seq // tm),
        in_specs=[row(d), row(RET_V_WIDTH), row(RET_V_WIDTH), row(RET_V_WIDTH), row(MLA_OUT_WIDTH),
                  row(RET_V_WIDTH), row(RET_V_WIDTH), _const_spec((1, RET_V_WIDTH)), _mod_spec(d, k_gate),
                  _const_spec(wr.shape), _const_spec(wm.shape), _const_spec(wo.shape)],
        out_specs=row(d),
        out_shape=jax.ShapeDtypeStruct(x.shape, F32),
        scratch_shapes=[pltpu.VMEM((tm, RET_V_WIDTH), BF16)],
        compiler_params=_cparams(("parallel", "parallel")),
        name="merge",
    )(x, o_f, o_b, rg, att, gr, gm, gn.reshape(1, -1), mod, wr, wm, wo)


def _rope_tables(length, n_freq):
    rows = length // GRID_W
    row = jnp.repeat(jnp.arange(rows, dtype=F32), GRID_W)
    col = jnp.tile(jnp.arange(GRID_W, dtype=F32), rows)
    inv_freq = jnp.power(ROPE_BASE, -jnp.arange(n_freq, dtype=F32) / n_freq)
    ang_r = row[:, None] * inv_freq[None, :]
    ang_c = col[:, None] * inv_freq[None, :]
    cos = jnp.concatenate([jnp.cos(ang_r)] * 2 + [jnp.cos(ang_c)] * 2, axis=-1)
    sin = jnp.concatenate([-jnp.sin(ang_r), jnp.sin(ang_r), -jnp.sin(ang_c), jnp.sin(ang_c)], axis=-1)
    return cos, sin


def _lane_tables(length, rope):
    if rope:
        rc, rs = _rope_tables(length, RET_QK_DIM // 4)
        mc, ms = _rope_tables(length, MLA_ROPE_DIM // 4)
    else:
        rc, rs = jnp.ones((length, RET_QK_DIM), F32), jnp.zeros((length, RET_QK_DIM), F32)
        mc, ms = jnp.ones((length, MLA_ROPE_DIM), F32), jnp.zeros((length, MLA_ROPE_DIM), F32)
    reps = LANES // RET_QK_DIM
    tail = LANES - MLA_NOPE_DIM - MLA_ROPE_DIM
    mcos = jnp.concatenate([jnp.ones((length, MLA_NOPE_DIM), F32), mc, jnp.ones((length, tail), F32)], axis=-1)
    msin = jnp.concatenate([jnp.zeros((length, MLA_NOPE_DIM), F32), ms, jnp.zeros((length, tail), F32)], axis=-1)
    return jnp.tile(rc, (1, reps)), jnp.tile(rs, (1, reps)), mcos, msin


def _prep_w_in(w_in):
    d = w_in.shape[0]
    o_kr = 2 * RET_QK_WIDTH + 2 * RET_V_WIDTH + MLA_Q_RANK + MLA_KV_RANK
    kr = jnp.concatenate([jnp.zeros((d, MLA_NOPE_DIM), F32), w_in[:, o_kr:o_kr + MLA_ROPE_DIM],
                          jnp.zeros((d, LANES - MLA_NOPE_DIM - MLA_ROPE_DIM), F32)], axis=1)
    return jnp.concatenate([w_in[:, :o_kr], kr, w_in[:, o_kr + MLA_ROPE_DIM:]], axis=1).astype(BF16)


def _prep_w_uq(w_uq):
    r = w_uq.shape[0]
    w = w_uq.reshape(r, MLA_HEADS, MLA_QK_DIM)
    w = jnp.pad(w, ((0, 0), (0, 0), (0, LANES - MLA_QK_DIM)))
    return w.reshape(r, MLA_HEADS * LANES).astype(BF16)


def _prep_w_ukv(w_ukv):
    r = w_ukv.shape[0]
    w = w_ukv.reshape(r, MLA_HEADS, MLA_NOPE_DIM + MLA_V_DIM)
    wk = jnp.pad(w[..., :MLA_NOPE_DIM], ((0, 0), (0, 0), (0, LANES - MLA_NOPE_DIM)))
    wv = w[..., MLA_NOPE_DIM:]
    return jnp.concatenate([wk.reshape(r, MLA_HEADS * LANES), wv.reshape(r, MLA_OUT_WIDTH)], axis=1).astype(BF16)


def kernel(x, c, ctx, c_ctx, w_ada, b_ada, ffn1_w1, ffn1_w3, ffn1_w2, ffn2_w1, ffn2_w3, ffn2_w2, w_in,
           ret_decay_fwd, ret_decay_bwd, ret_gn, mla_q_norm, mla_kv_norm, w_uq, w_ukv, w_ret_out,
           w_mla_out, w_o, final_norm):
    bsz, seq, d = x.shape
    lc = ctx.shape[1]
    depth = w_ada.shape[0]

    rows = -(-(bsz + 1) // 8) * 8
    cc = jnp.zeros((rows, d), F32).at[:bsz].set(c).at[bsz].set(c_ctx)
    mod_all = _mod_call(cc, w_ada, b_ada)

    tab_lat = _lane_tables(seq, True)
    tab_ctx = _lane_tables(lc, False)
    zero_state = jnp.zeros((bsz, RET_HEADS, RET_QK_DIM, RET_V_DIM), F32)

    xc = ctx
    for l in range(depth):
        last = l == depth - 1
        mod = mod_all[l, :bsz][:, None, :]
        mod_c = jnp.broadcast_to(mod_all[l, bsz][None, None, :], (bsz, 1, N_MOD * d))
        f1 = (ffn1_w1[l].astype(BF16), ffn1_w3[l].astype(BF16), ffn1_w2[l].astype(BF16))
        f2 = (ffn2_w1[l].astype(BF16), ffn2_w3[l].astype(BF16), ffn2_w2[l].astype(BF16))
        w_in_l = _prep_w_in(w_in[l])
        wuq_l = _prep_w_uq(w_uq[l])
        wukv_l = _prep_w_ukv(w_ukv[l])
        wr, wm, wo = w_ret_out[l].astype(BF16), w_mla_out[l].astype(BF16), w_o[l].astype(BF16)

        x = _ffn_call(x, mod, 0, *f1)
        xc = _ffn_call(xc, mod_c, 0, *f1)

        p_lat = _inproj_call(x, mod, 3, w_in_l, mla_q_norm[l], mla_kv_norm[l], wuq_l, wukv_l, tab_lat)
        p_ctx = _inproj_call(xc, mod_c, 3, w_in_l, mla_q_norm[l], mla_kv_norm[l], wuq_l, wukv_l, tab_ctx)
        rq, rk, rv, rg, qm, km, vm, gr, gm = p_lat
        crq, crk, crv, crg, cqm, ckm, cvm, cgr, cgm = p_ctx

        oc_f, oc_b, sc_f, sc_b = _ret_call(crq, crk, crv, ret_decay_fwd[l], ret_decay_bwd[l],
                                           zero_state, zero_state)
        ol_f, ol_b, _, _ = _ret_call(rq, rk, rv, ret_decay_fwd[l], ret_decay_bwd[l], sc_f, sc_b)

        att = _attn_call(qm, jnp.concatenate([km, ckm], axis=1), jnp.concatenate([vm, cvm], axis=1))
        x = _merge_call(x, ol_f, ol_b, rg, att, gr, gm, ret_gn[l], mod, 5, wr, wm, wo)
        x = _ffn_call(x, mod, 6, *f2, final_gain=final_norm if last else None)
        if not last:
            att_c = _attn_call(cqm, ckm, cvm)
            xc = _merge_call(xc, oc_f, oc_b, crg, att_c, cgr, cgm, ret_gn[l], mod_c, 5, wr, wm, wo)
            xc = _ffn_call(xc, mod_c, 6, *f2)
    return x
```

```python
import functools

import jax
import jax.numpy as jnp
from jax import lax
from jax.experimental import pallas as pl
from jax.experimental.pallas import tpu as pltpu

F32 = jnp.float32
BF16 = jnp.bfloat16

GRID_W = 64
EPS = 1e-6
ROPE_BASE = 10000.0
N_MOD = 9

RET_HEADS = 8
RET_QK_DIM = 64
RET_V_DIM = 128
RET_CHUNK = 128
RET_QK_WIDTH = RET_HEADS * RET_QK_DIM
RET_V_WIDTH = RET_HEADS * RET_V_DIM
RET_SCALE = RET_QK_DIM ** -0.5

MLA_HEADS = 8
MLA_Q_RANK = 384
MLA_KV_RANK = 256
MLA_NOPE_DIM = 64
MLA_ROPE_DIM = 32
MLA_V_DIM = 64
MLA_QK_DIM = MLA_NOPE_DIM + MLA_ROPE_DIM
MLA_OUT_WIDTH = MLA_HEADS * MLA_V_DIM
MLA_SCALE = MLA_QK_DIM ** -0.5

LANES = 128
MLA_HEAD_PAD = LANES
VMEM_LIMIT = 56 * 1024 * 1024
BF16_SUBLANES = 16
_VT_ROWS = MLA_V_DIM + BF16_SUBLANES
_Q_SCALE = MLA_SCALE * 1.4426950408889634
_NT = (((1,), (1,)), ((), ()))

_O_RQ = 0
_O_RK = _O_RQ + RET_QK_WIDTH
_O_RV = _O_RK + RET_QK_WIDTH
_O_RG = _O_RV + RET_V_WIDTH
_O_DQ = _O_RG + RET_V_WIDTH
_O_DKV = _O_DQ + MLA_Q_RANK
_O_KR = _O_DKV + MLA_KV_RANK
_O_GR = _O_KR + LANES
_O_GM = _O_GR + RET_V_WIDTH
_IN_W = _O_GM + RET_V_WIDTH


def _cparams(sem, vmem=VMEM_LIMIT):
    return pltpu.CompilerParams(dimension_semantics=sem, vmem_limit_bytes=vmem)


def _const_spec(shape):
    n = len(shape)
    return pl.BlockSpec(shape, lambda *_: (0,) * n, pipeline_mode=pl.Buffered(1))


def _rms(xf):
    return xf * lax.rsqrt(jnp.mean(xf * xf, axis=-1, keepdims=True) + EPS)


def _silu(a):
    return a * jax.nn.sigmoid(a)


def _mod_kernel(c_ref, w_ref, b_ref, o_ref):
    s = _silu(c_ref[...])
    o_ref[0] = jnp.dot(s, w_ref[0], preferred_element_type=F32,
                       precision=lax.Precision.HIGHEST) + b_ref[0]


def _mod_call(cc, w_ada, b_ada):
    depth, d, nd = w_ada.shape
    rows = cc.shape[0]
    tn = 1024
    return pl.pallas_call(
        _mod_kernel,
        grid=(depth, nd // tn),
        in_specs=[
            pl.BlockSpec((rows, d), lambda l, j: (0, 0)),
            pl.BlockSpec((1, d, tn), lambda l, j: (l, 0, j)),
            pl.BlockSpec((1, 1, tn), lambda l, j: (l, 0, j)),
        ],
        out_specs=pl.BlockSpec((1, rows, tn), lambda l, j: (l, 0, j)),
        out_shape=jax.ShapeDtypeStruct((depth, rows, nd), F32),
        compiler_params=_cparams(("arbitrary", "arbitrary")),
        name="mod",
    )(cc, w_ada, b_ada.reshape(depth, 1, nd))


def _ffn_kernel(x_ref, sh_ref, sc_ref, g_ref, w1_ref, w3_ref, w2_ref, *rest, final):
    if final:
        fn_ref, o_ref = rest
    else:
        (o_ref,) = rest
    xf = x_ref[0]
    h = _rms(xf) * (1.0 + sc_ref[0]) + sh_ref[0]
    hb = h.astype(BF16)
    a = jnp.dot(hb, w1_ref[...], preferred_element_type=F32)
    b = jnp.dot(hb, w3_ref[...], preferred_element_type=F32)
    u = (_silu(a) * b).astype(BF16)
    y = jnp.dot(u, w2_ref[...], preferred_element_type=F32)
    xn = xf + (0.5 * g_ref[0]) * y
    if final:
        xn = _rms(xn) * fn_ref[...]
    o_ref[0] = xn


def _mod_spec(d, k):
    return pl.BlockSpec((1, 1, d), lambda b, i: (b, 0, k))


def _ffn_call(x, mod, k0, w1, w3, w2, final_gain=None):
    bsz, seq, d = x.shape
    dff = w1.shape[1]
    tm = min(512, seq)
    final = final_gain is not None
    in_specs = [
        pl.BlockSpec((1, tm, d), lambda b, i: (b, i, 0)),
        _mod_spec(d, k0), _mod_spec(d, k0 + 1), _mod_spec(d, k0 + 2),
        _const_spec((d, dff)), _const_spec((d, dff)), _const_spec((dff, d)),
    ]
    args = [x, mod, mod, mod, w1, w3, w2]
    if final:
        in_specs.append(_const_spec((1, d)))
        args.append(final_gain.reshape(1, d))
    return pl.pallas_call(
        functools.partial(_ffn_kernel, final=final),
        grid=(bsz, seq // tm),
        in_specs=in_specs,
        out_specs=pl.BlockSpec((1, tm, d), lambda b, i: (b, i, 0)),
        out_shape=jax.ShapeDtypeStruct(x.shape, F32),
        compiler_params=_cparams(("parallel", "parallel")),
        name="ffn_final" if final else "ffn",
    )(*args)


def _rope(y, cos, sin, first_half, off):
    partner = jnp.where(first_half, pltpu.roll(y, LANES - off, 1), pltpu.roll(y, off, 1))
    return y * cos + partner * sin


def _inproj_kernel(x_ref, sh_ref, sc_ref, w_ref, qn_ref, kvn_ref, wuqt_ref, wuk_ref, wvt_ref,
                   rcos_ref, rsin_ref, mcos_ref, msin_ref, mtab_ref,
                   rq_ref, rk_ref, rv_ref, rg_ref, qt_ref, km_ref, vt_ref, gr_ref, gm_ref):
    tm = x_ref.shape[1]
    xf = x_ref[0]
    hb = (_rms(xf) * (1.0 + sc_ref[0]) + sh_ref[0]).astype(BF16)

    def proj(a, b):
        return jnp.dot(hb, w_ref[:, a:b], preferred_element_type=F32)

    lane = lax.broadcasted_iota(jnp.int32, (tm, LANES), 1)
    ret_first = (lane % (RET_QK_DIM // 2)) < (RET_QK_DIM // 4)
    mla_first = (lane % (MLA_ROPE_DIM // 2)) < (MLA_ROPE_DIM // 4)
    rcos, rsin = rcos_ref[...], rsin_ref[...]

    q = proj(_O_RQ, _O_RK)
    k = proj(_O_RK, _O_RV)
    for g in range(RET_QK_WIDTH // LANES):
        sl = slice(g * LANES, (g + 1) * LANES)
        rq_ref[0, :, sl] = _rope(q[:, sl], rcos, rsin, ret_first, RET_QK_DIM // 4).astype(BF16)
        rk_ref[0, :, sl] = (_rope(k[:, sl], rcos, rsin, ret_first, RET_QK_DIM // 4) * RET_SCALE).astype(BF16)
    rv_ref[0] = proj(_O_RV, _O_RG).astype(BF16)
    rg_ref[0] = proj(_O_RG, _O_DQ)

    dq = (_rms(proj(_O_DQ, _O_DKV)) * qn_ref[...]).astype(BF16)
    qt = lax.dot_general(wuqt_ref[...], dq, _NT, preferred_element_type=F32) * _Q_SCALE
    nf = MLA_ROPE_DIM // 4
    cos_r, sin_r = mtab_ref[0:nf, :], mtab_ref[nf:2 * nf, :]
    cos_c, sin_c = mtab_ref[2 * nf:3 * nf, :], mtab_ref[3 * nf:4 * nf, :]
    for h in range(MLA_HEADS):
        r0 = h * LANES + MLA_NOPE_DIM
        a, b = qt[r0:r0 + nf], qt[r0 + nf:r0 + 2 * nf]
        c, d = qt[r0 + 2 * nf:r0 + 3 * nf], qt[r0 + 3 * nf:r0 + 4 * nf]
        head = jnp.concatenate([
            qt[h * LANES:r0],
            a * cos_r - b * sin_r, b * cos_r + a * sin_r,
            c * cos_c - d * sin_c, d * cos_c + c * sin_c,
            qt[r0 + MLA_ROPE_DIM:(h + 1) * LANES]], axis=0)
        qt_ref[0, h * LANES:(h + 1) * LANES, :] = head.astype(BF16)

    dkv = (_rms(proj(_O_DKV, _O_KR)) * kvn_ref[...]).astype(BF16)
    kn = jnp.dot(dkv, wuk_ref[...], preferred_element_type=F32)
    kr = _rope(proj(_O_KR, _O_GR), mcos_ref[...], msin_ref[...], mla_first, MLA_ROPE_DIM // 4)
    for h in range(MLA_HEADS):
        sl = slice(h * LANES, (h + 1) * LANES)
        km_ref[0, :, sl] = (kn[:, sl] + kr).astype(BF16)
    vt = lax.dot_general(wvt_ref[...], dkv, _NT, preferred_element_type=F32)
    pad = _VT_ROWS - MLA_V_DIM
    ones_row = (lax.broadcasted_iota(jnp.int32, (pad, tm), 0) == 0).astype(BF16)
    for h in range(MLA_HEADS):
        vt_ref[0, h * _VT_ROWS:h * _VT_ROWS + MLA_V_DIM, :] = vt[h * MLA_V_DIM:(h + 1) * MLA_V_DIM].astype(BF16)
        vt_ref[0, h * _VT_ROWS + MLA_V_DIM:(h + 1) * _VT_ROWS, :] = ones_row

    gr_ref[0] = proj(_O_GR, _O_GM)
    gm_ref[0] = proj(_O_GM, _IN_W)


def _inproj_call(x, mod, k0, w, q_norm, kv_norm, wuqt, wuk, wvt, tables):
    bsz, seq, d = x.shape
    tm = min(256, seq)
    row = lambda w_: pl.BlockSpec((1, tm, w_), lambda b, i: (b, i, 0))
    col = lambda r_: pl.BlockSpec((1, r_, tm), lambda b, i: (b, 0, i))
    tab = pl.BlockSpec((tm, LANES), lambda b, i: (i, 0))
    ttab = pl.BlockSpec((MLA_ROPE_DIM, tm), lambda b, i: (0, i))
    n_q, n_vt = MLA_HEADS * LANES, MLA_HEADS * _VT_ROWS
    shapes = [(seq, RET_QK_WIDTH), (seq, RET_QK_WIDTH), (seq, RET_V_WIDTH), (seq, RET_V_WIDTH),
              (n_q, seq), (seq, n_q), (n_vt, seq), (seq, RET_V_WIDTH), (seq, RET_V_WIDTH)]
    dtypes = (BF16, BF16, BF16, F32, BF16, BF16, BF16, F32, F32)
    out_specs = [col(s[0]) if i in (4, 6) else row(s[1]) for i, s in enumerate(shapes)]
    return pl.pallas_call(
        _inproj_kernel,
        grid=(bsz, seq // tm),
        in_specs=[
            row(d), _mod_spec(d, k0), _mod_spec(d, k0 + 1),
            _const_spec(w.shape), _const_spec((1, MLA_Q_RANK)), _const_spec((1, MLA_KV_RANK)),
            _const_spec(wuqt.shape), _const_spec(wuk.shape), _const_spec(wvt.shape),
            tab, tab, tab, tab, ttab,
        ],
        out_specs=out_specs,
        out_shape=[jax.ShapeDtypeStruct((bsz,) + s, dt) for s, dt in zip(shapes, dtypes)],
        compiler_params=_cparams(("parallel", "parallel")),
        name="inproj",
    )(x, mod, mod, w, q_norm.reshape(1, -1), kv_norm.reshape(1, -1), wuqt, wuk, wvt, *tables)


def _ret_kernel(qf_ref, kf_ref, vf_ref, qb_ref, kb_ref, vb_ref, decf_ref, decb_ref, sf0_ref, sb0_ref,
                of_ref, ob_ref, sf_ref, sb_ref,
                st_f, st_b, dm_f, dm_b, qs_f, qs_b, ks_f, ks_b, cd_f, cd_b):
    n = pl.program_id(1)
    c = qf_ref.shape[1]
    dk, dv = RET_QK_DIM, RET_V_DIM

    @pl.when(n == 0)
    def _():
        st_f[...] = sf0_ref[0]
        st_b[...] = sb0_ref[0]
        ii = lax.broadcasted_iota(jnp.int32, (c, c), 0).astype(F32)
        jj = lax.broadcasted_iota(jnp.int32, (c, c), 1).astype(F32)
        diff = ii - jj
        rv_ = lax.broadcasted_iota(jnp.int32, (c, dv), 0).astype(F32)
        rk_ = lax.broadcasted_iota(jnp.int32, (c, dk), 0).astype(F32)
        for h in range(RET_HEADS):
            lgf = -jnp.exp(decf_ref[h])
            lgb = -jnp.exp(decb_ref[h])
            dm_f[h] = jnp.where(diff >= 0, jnp.exp(lgf * jnp.maximum(diff, 0.0)), 0.0)
            dm_b[h] = jnp.where(diff <= 0, jnp.exp(lgb * jnp.maximum(-diff, 0.0)), 0.0)
            qs_f[h] = jnp.exp(lgf * (rv_ + 1.0))
            qs_b[h] = jnp.exp(lgb * (c - rv_))
            ks_f[h] = jnp.exp(lgf * (c - 1.0 - rk_))
            ks_b[h] = jnp.exp(lgb * rk_)
            cd_f[h] = jnp.broadcast_to(jnp.exp(lgf * c), (1, dv))
            cd_b[h] = jnp.broadcast_to(jnp.exp(lgb * c), (1, dv))

    def one_dir(q_ref, k_ref, v_ref, o_ref, st, dm, qs, ks, cd):
        for h in range(RET_HEADS):
            q = q_ref[0, :, h * dk:(h + 1) * dk]
            k = k_ref[0, :, h * dk:(h + 1) * dk]
            v = v_ref[0, :, h * dv:(h + 1) * dv]
            s = lax.dot_general(q, k, (((1,), (1,)), ((), ())), preferred_element_type=F32)
            p = (s * dm[h]).astype(BF16)
            state = st[h]
            o = jnp.dot(p, v, preferred_element_type=F32)
            o += qs[h] * jnp.dot(q, state.astype(BF16), preferred_element_type=F32)
            o_ref[0, :, h * dv:(h + 1) * dv] = o
            kd = (k.astype(F32) * ks[h]).astype(BF16)
            st[h] = cd[h] * state + lax.dot_general(
                kd, v, (((0,), (0,)), ((), ())), preferred_element_type=F32)

    one_dir(qf_ref, kf_ref, vf_ref, of_ref, st_f, dm_f, qs_f, ks_f, cd_f)
    one_dir(qb_ref, kb_ref, vb_ref, ob_ref, st_b, dm_b, qs_b, ks_b, cd_b)

    @pl.when(n == pl.num_programs(1) - 1)
    def _():
        sf_ref[0] = st_f[...]
        sb_ref[0] = st_b[...]


def _ret_call(rq, rk, rv, dec_f, dec_b, s0_f, s0_b):
    bsz, seq, _ = rq.shape
    c = RET_CHUNK
    nc = seq // c
    h, dk, dv = RET_HEADS, RET_QK_DIM, RET_V_DIM
    fwd = lambda w_: pl.BlockSpec((1, c, w_), lambda b, n: (b, n, 0))
    bwd = lambda w_: pl.BlockSpec((1, c, w_), lambda b, n: (b, nc - 1 - n, 0))
    st_spec = pl.BlockSpec((1, h, dk, dv), lambda b, n: (b, 0, 0, 0))
    dec_spec = pl.BlockSpec((h, 1, 1), lambda b, n: (0, 0, 0))
    return pl.pallas_call(
        _ret_kernel,
        grid=(bsz, nc),
        in_specs=[fwd(h * dk), fwd(h * dk), fwd(h * dv), bwd(h * dk), bwd(h * dk), bwd(h * dv),
                  dec_spec, dec_spec, st_spec, st_spec],
        out_specs=[fwd(h * dv), bwd(h * dv), st_spec, st_spec],
        out_shape=[jax.ShapeDtypeStruct((bsz, seq, h * dv), F32)] * 2
        + [jax.ShapeDtypeStruct((bsz, h, dk, dv), F32)] * 2,
        scratch_shapes=[
            pltpu.VMEM((h, dk, dv), F32), pltpu.VMEM((h, dk, dv), F32),
            pltpu.VMEM((h, c, c), F32), pltpu.VMEM((h, c, c), F32),
            pltpu.VMEM((h, c, dv), F32), pltpu.VMEM((h, c, dv), F32),
            pltpu.VMEM((h, c, dk), F32), pltpu.VMEM((h, c, dk), F32),
            pltpu.VMEM((h, 1, dv), F32), pltpu.VMEM((h, 1, dv), F32),
        ],
        compiler_params=_cparams(("parallel", "arbitrary")),
        name="retention",
    )(rq, rk, rv, rq, rk, rv, dec_f.reshape(h, 1, 1), dec_b.reshape(h, 1, 1), s0_f, s0_b)


def _attn_kernel(qt_ref, k_ref, vt_ref, o_ref, *, tk, depth):
    tq = qt_ref.shape[2]
    nk = k_ref.shape[1] // tk
    heads = range(2)
    qt = [qt_ref[0, hd * LANES:(hd + 1) * LANES, :] for hd in heads]

    def scores(hd, j):
        k = k_ref[0, j * tk:(j + 1) * tk, hd * LANES:(hd + 1) * LANES]
        return jnp.dot(k, qt[hd], preferred_element_type=F32)

    m = [jnp.full((1, tq), -jnp.inf, F32) for _ in heads]
    acc = [jnp.zeros((_VT_ROWS, tq), F32) for _ in heads]
    pending = {}
    for j in range(-depth, nk):
        if j + depth < nk:
            for hd in heads:
                pending[hd, j + depth] = scores(hd, j + depth)
        if j < 0:
            continue
        for hd in heads:
            st = pending.pop((hd, j))
            vt = vt_ref[0, hd * _VT_ROWS:(hd + 1) * _VT_ROWS, j * tk:(j + 1) * tk]
            m_new = jnp.maximum(m[hd], jnp.max(st, axis=0, keepdims=True))
            alpha = jnp.exp2(m[hd] - m_new)
            pt = jnp.exp2(st - m_new).astype(BF16)
            acc[hd] = alpha * acc[hd] + jnp.dot(vt, pt, preferred_element_type=F32)
            m[hd] = m_new
    outs = [a[:MLA_V_DIM] / a[MLA_V_DIM:MLA_V_DIM + 1] for a in acc]
    o_ref[0] = jnp.concatenate(outs, axis=0).T.astype(o_ref.dtype)


def _attn_call(qt, km, vt):
    bsz, _, seq = qt.shape
    lk = km.shape[1]
    tq = min(256, seq)
    tk = 256
    return pl.pallas_call(
        functools.partial(_attn_kernel, tk=tk, depth=2),
        grid=(bsz, MLA_HEADS // 2, seq // tq),
        in_specs=[
            pl.BlockSpec((1, 2 * LANES, tq), lambda b, h, i: (b, h, i)),
            pl.BlockSpec((1, lk, 2 * LANES), lambda b, h, i: (b, 0, h)),
            pl.BlockSpec((1, 2 * _VT_ROWS, lk), lambda b, h, i: (b, h, 0)),
        ],
        out_specs=pl.BlockSpec((1, tq, LANES), lambda b, h, i: (b, i, h)),
        out_shape=jax.ShapeDtypeStruct((bsz, seq, MLA_OUT_WIDTH), BF16),
        compiler_params=_cparams(("parallel", "parallel", "parallel")),
        name="attention",
    )(qt, km, vt)


def _merge_kernel(x_ref, of_ref, ob_ref, rg_ref, at_ref, gr_ref, gm_ref, gn_ref, g2_ref,
                  wr_ref, wm_ref, wo_ref, o_ref, a_scr):
    dv = RET_V_DIM
    for h in range(RET_HEADS):
        sl = slice(h * dv, (h + 1) * dv)
        o = of_ref[0, :, sl] + ob_ref[0, :, sl]
        d = o - jnp.mean(o, axis=-1, keepdims=True)
        nrm = d * lax.rsqrt(jnp.mean(d * d, axis=-1, keepdims=True) + EPS) * gn_ref[:, sl]
        a_scr[:, sl] = (_silu(rg_ref[0, :, sl]) * nrm).astype(BF16)
    ret_l = jnp.dot(a_scr[...], wr_ref[...], preferred_element_type=F32)
    mla_l = jnp.dot(at_ref[0], wm_ref[...], preferred_element_type=F32)
    mrg = jax.nn.sigmoid(gr_ref[0]) * ret_l + jax.nn.sigmoid(gm_ref[0]) * mla_l
    out = jnp.dot(mrg.astype(BF16), wo_ref[...], preferred_element_type=F32)
    o_ref[0] = x_ref[0] + g2_ref[0] * out


def _merge_call(x, o_f, o_b, rg, att, gr, gm, gn, mod, k_gate, wr, wm, wo):
    bsz, seq, d = x.shape
    tm = min(256, seq)
    row = lambda w_: pl.BlockSpec((1, tm, w_), lambda b, i: (b, i, 0))
    return pl.pallas_call(
        _merge_kernel,
        grid=(bsz, seq // tm),
        in_specs=[row(d), row(RET_V_WIDTH), row(RET_V_WIDTH), row(RET_V_WIDTH), row(MLA_OUT_WIDTH),
                  row(RET_V_WIDTH), row(RET_V_WIDTH), _const_spec((1, RET_V_WIDTH)), _mod_spec(d, k_gate),
                  _const_spec(wr.shape), _const_spec(wm.shape), _const_spec(wo.shape)],
        out_specs=row(d),
        out_shape=jax.ShapeDtypeStruct(x.shape, F32),
        scratch_shapes=[pltpu.VMEM((tm, RET_V_WIDTH), BF16)],
        compiler_params=_cparams(("parallel", "parallel")),
        name="merge",
    )(x, o_f, o_b, rg, att, gr, gm, gn.reshape(1, -1), mod, wr, wm, wo)


def _rope_tables(length, n_freq):
    rows = length // GRID_W
    row = jnp.repeat(jnp.arange(rows, dtype=F32), GRID_W)
    col = jnp.tile(jnp.arange(GRID_W, dtype=F32), rows)
    inv_freq = jnp.power(ROPE_BASE, -jnp.arange(n_freq, dtype=F32) / n_freq)
    ang_r = row[:, None] * inv_freq[None, :]
    ang_c = col[:, None] * inv_freq[None, :]
    return jnp.cos(ang_r), jnp.sin(ang_r), jnp.cos(ang_c), jnp.sin(ang_c)


def _pair_tables(t):
    cos_r, sin_r, cos_c, sin_c = t
    return (jnp.concatenate([cos_r, cos_r, cos_c, cos_c], axis=-1),
            jnp.concatenate([-sin_r, sin_r, -sin_c, sin_c], axis=-1))


def _tables(length, rope):
    nr, nm = RET_QK_DIM // 4, MLA_ROPE_DIM // 4
    if rope:
        rt, mt = _rope_tables(length, nr), _rope_tables(length, nm)
    else:
        one, zero = jnp.ones((length, 1), F32), jnp.zeros((length, 1), F32)
        rt = tuple(jnp.tile(v, (1, nr)) for v in (one, zero, one, zero))
        mt = tuple(jnp.tile(v, (1, nm)) for v in (one, zero, one, zero))
    rc, rs = _pair_tables(rt)
    mc, ms = _pair_tables(mt)
    reps = LANES // RET_QK_DIM
    tail = LANES - MLA_NOPE_DIM - MLA_ROPE_DIM
    mcos = jnp.concatenate([jnp.ones((length, MLA_NOPE_DIM), F32), mc, jnp.ones((length, tail), F32)], axis=-1)
    msin = jnp.concatenate([jnp.zeros((length, MLA_NOPE_DIM), F32), ms, jnp.zeros((length, tail), F32)], axis=-1)
    mtab = jnp.concatenate(mt, axis=-1).T
    return jnp.tile(rc, (1, reps)), jnp.tile(rs, (1, reps)), mcos, msin, mtab


def _prep_w_in(w_in):
    d = w_in.shape[0]
    o_kr = 2 * RET_QK_WIDTH + 2 * RET_V_WIDTH + MLA_Q_RANK + MLA_KV_RANK
    kr = jnp.concatenate([jnp.zeros((d, MLA_NOPE_DIM), F32), w_in[:, o_kr:o_kr + MLA_ROPE_DIM],
                          jnp.zeros((d, LANES - MLA_NOPE_DIM - MLA_ROPE_DIM), F32)], axis=1)
    return jnp.concatenate([w_in[:, :o_kr], kr, w_in[:, o_kr + MLA_ROPE_DIM:]], axis=1).astype(BF16)


def _prep_w_uq(w_uq):
    r = w_uq.shape[0]
    w = w_uq.reshape(r, MLA_HEADS, MLA_QK_DIM)
    w = jnp.pad(w, ((0, 0), (0, 0), (0, LANES - MLA_QK_DIM)))
    return w.reshape(r, MLA_HEADS * LANES).T.astype(BF16)


def _prep_w_ukv(w_ukv):
    r = w_ukv.shape[0]
    w = w_ukv.reshape(r, MLA_HEADS, MLA_NOPE_DIM + MLA_V_DIM)
    wk = jnp.pad(w[..., :MLA_NOPE_DIM], ((0, 0), (0, 0), (0, LANES - MLA_NOPE_DIM)))
    wv = w[..., MLA_NOPE_DIM:]
    return wk.reshape(r, MLA_HEADS * LANES).astype(BF16), wv.reshape(r, MLA_OUT_WIDTH).T.astype(BF16)


def kernel(x, c, ctx, c_ctx, w_ada, b_ada, ffn1_w1, ffn1_w3, ffn1_w2, ffn2_w1, ffn2_w3, ffn2_w2, w_in,
           ret_decay_fwd, ret_decay_bwd, ret_gn, mla_q_norm, mla_kv_norm, w_uq, w_ukv, w_ret_out,
           w_mla_out, w_o, final_norm):
    bsz, seq, d = x.shape
    lc = ctx.shape[1]
    depth = w_ada.shape[0]

    rows = -(-(bsz + 1) // 8) * 8
    cc = jnp.zeros((rows, d), F32).at[:bsz].set(c).at[bsz].set(c_ctx)
    mod_all = _mod_call(cc, w_ada, b_ada)

    tab_lat = _tables(seq, True)
    tab_ctx = _tables(lc, False)
    zero_state = jnp.zeros((bsz, RET_HEADS, RET_QK_DIM, RET_V_DIM), F32)

    xc = ctx
    for l in range(depth):
        last = l == depth - 1
        mod = mod_all[l, :bsz][:, None, :]
        mod_c = jnp.broadcast_to(mod_all[l, bsz][None, None, :], (bsz, 1, N_MOD * d))
        f1 = (ffn1_w1[l].astype(BF16), ffn1_w3[l].astype(BF16), ffn1_w2[l].astype(BF16))
        f2 = (ffn2_w1[l].astype(BF16), ffn2_w3[l].astype(BF16), ffn2_w2[l].astype(BF16))
        w_in_l = _prep_w_in(w_in[l])
        wuqt_l = _prep_w_uq(w_uq[l])
        wuk_l, wvt_l = _prep_w_ukv(w_ukv[l])
        wr, wm, wo = w_ret_out[l].astype(BF16), w_mla_out[l].astype(BF16), w_o[l].astype(BF16)

        x = _ffn_call(x, mod, 0, *f1)
        xc = _ffn_call(xc, mod_c, 0, *f1)

        mla_w = (w_in_l, mla_q_norm[l], mla_kv_norm[l], wuqt_l, wuk_l, wvt_l)
        rq, rk, rv, rg, qt, km, vt, gr, gm = _inproj_call(x, mod, 3, *mla_w, tab_lat)
        crq, crk, crv, crg, cqt, ckm, cvt, cgr, cgm = _inproj_call(xc, mod_c, 3, *mla_w, tab_ctx)

        oc_f, oc_b, sc_f, sc_b = _ret_call(crq, crk, crv, ret_decay_fwd[l], ret_decay_bwd[l],
                                           zero_state, zero_state)
        ol_f, ol_b, _, _ = _ret_call(rq, rk, rv, ret_decay_fwd[l], ret_decay_bwd[l], sc_f, sc_b)

        att = _attn_call(qt, jnp.concatenate([km, ckm], axis=1), jnp.concatenate([vt, cvt], axis=2))
        x = _merge_call(x, ol_f, ol_b, rg, att, gr, gm, ret_gn[l], mod, 5, wr, wm, wo)
        x = _ffn_call(x, mod, 6, *f2, final_gain=final_norm if last else None)
        if not last:
            att_c = _attn_call(cqt, ckm, cvt)
            xc = _merge_call(xc, oc_f, oc_b, crg, att_c, cgr, cgm, ret_gn[l], mod_c, 5, wr, wm, wo)
            xc = _ffn_call(xc, mod_c, 6, *f2)
    return x
```

```python
import functools

import jax
import jax.numpy as jnp
from jax import lax
from jax.experimental import pallas as pl
from jax.experimental.pallas import tpu as pltpu

F32 = jnp.float32
BF16 = jnp.bfloat16

GRID_W = 64
EPS = 1e-6
ROPE_BASE = 10000.0
N_MOD = 9

RET_HEADS = 8
RET_QK_DIM = 64
RET_V_DIM = 128
RET_CHUNK = 128
RET_QK_WIDTH = RET_HEADS * RET_QK_DIM
RET_V_WIDTH = RET_HEADS * RET_V_DIM
RET_SCALE = RET_QK_DIM ** -0.5

MLA_HEADS = 8
MLA_Q_RANK = 384
MLA_KV_RANK = 256
MLA_NOPE_DIM = 64
MLA_ROPE_DIM = 32
MLA_V_DIM = 64
MLA_QK_DIM = MLA_NOPE_DIM + MLA_ROPE_DIM
MLA_OUT_WIDTH = MLA_HEADS * MLA_V_DIM
MLA_SCALE = MLA_QK_DIM ** -0.5

LANES = 128
MLA_HEAD_PAD = LANES
VMEM_LIMIT = 56 * 1024 * 1024
BF16_SUBLANES = 16
_VT_ROWS = MLA_V_DIM + BF16_SUBLANES
_Q_SCALE = MLA_SCALE * 1.4426950408889634
_NT = (((1,), (1,)), ((), ()))

_O_RQ = 0
_O_RK = _O_RQ + RET_QK_WIDTH
_O_RV = _O_RK + RET_QK_WIDTH
_O_RG = _O_RV + RET_V_WIDTH
_O_DQ = _O_RG + RET_V_WIDTH
_O_DKV = _O_DQ + MLA_Q_RANK
_O_KR = _O_DKV + MLA_KV_RANK
_O_GR = _O_KR + LANES
_O_GM = _O_GR + RET_V_WIDTH
_IN_W = _O_GM + RET_V_WIDTH


def _cparams(sem, vmem=VMEM_LIMIT):
    return pltpu.CompilerParams(dimension_semantics=sem, vmem_limit_bytes=vmem)


def _const_spec(shape):
    n = len(shape)
    return pl.BlockSpec(shape, lambda *_: (0,) * n, pipeline_mode=pl.Buffered(1))


def _rms(xf):
    return xf * lax.rsqrt(jnp.mean(xf * xf, axis=-1, keepdims=True) + EPS)


def _silu(a):
    return a * jax.nn.sigmoid(a)


def _mod_kernel(c_ref, w_ref, b_ref, o_ref):
    s = _silu(c_ref[...])
    o_ref[0] = jnp.dot(s, w_ref[0], preferred_element_type=F32,
                       precision=lax.Precision.HIGHEST) + b_ref[0]


def _mod_call(cc, w_ada, b_ada):
    depth, d, nd = w_ada.shape
    rows = cc.shape[0]
    tn = 1024
    return pl.pallas_call(
        _mod_kernel,
        grid=(depth, nd // tn),
        in_specs=[
            pl.BlockSpec((rows, d), lambda l, j: (0, 0)),
            pl.BlockSpec((1, d, tn), lambda l, j: (l, 0, j)),
            pl.BlockSpec((1, 1, tn), lambda l, j: (l, 0, j)),
        ],
        out_specs=pl.BlockSpec((1, rows, tn), lambda l, j: (l, 0, j)),
        out_shape=jax.ShapeDtypeStruct((depth, rows, nd), F32),
        compiler_params=_cparams(("arbitrary", "arbitrary")),
        name="mod",
    )(cc, w_ada, b_ada.reshape(depth, 1, nd))


def _ffn_kernel(x_ref, sh_ref, sc_ref, g_ref, w1_ref, w3_ref, w2_ref, *rest, final):
    if final:
        fn_ref, o_ref = rest
    else:
        (o_ref,) = rest
    xf = x_ref[0]
    h = _rms(xf) * (1.0 + sc_ref[0]) + sh_ref[0]
    hb = h.astype(BF16)
    a = jnp.dot(hb, w1_ref[...], preferred_element_type=F32)
    b = jnp.dot(hb, w3_ref[...], preferred_element_type=F32)
    u = (_silu(a) * b).astype(BF16)
    y = jnp.dot(u, w2_ref[...], preferred_element_type=F32)
    xn = xf + (0.5 * g_ref[0]) * y
    if final:
        xn = _rms(xn) * fn_ref[...]
    o_ref[0] = xn


def _mod_spec(d, k):
    return pl.BlockSpec((1, 1, d), lambda b, i: (b, 0, k))


def _ffn_call(x, mod, k0, w1, w3, w2, final_gain=None):
    bsz, seq, d = x.shape
    dff = w1.shape[1]
    tm = min(512, seq)
    final = final_gain is not None
    in_specs = [
        pl.BlockSpec((1, tm, d), lambda b, i: (b, i, 0)),
        _mod_spec(d, k0), _mod_spec(d, k0 + 1), _mod_spec(d, k0 + 2),
        _const_spec((d, dff)), _const_spec((d, dff)), _const_spec((dff, d)),
    ]
    args = [x, mod, mod, mod, w1, w3, w2]
    if final:
        in_specs.append(_const_spec((1, d)))
        args.append(final_gain.reshape(1, d))
    return pl.pallas_call(
        functools.partial(_ffn_kernel, final=final),
        grid=(bsz, seq // tm),
        in_specs=in_specs,
        out_specs=pl.BlockSpec((1, tm, d), lambda b, i: (b, i, 0)),
        out_shape=jax.ShapeDtypeStruct(x.shape, F32),
        compiler_params=_cparams(("parallel", "parallel")),
        name="ffn_final" if final else "ffn",
    )(*args)


def _rope(y, cos, sin, first_half, off):
    partner = jnp.where(first_half, pltpu.roll(y, LANES - off, 1), pltpu.roll(y, off, 1))
    return y * cos + partner * sin


def _inproj_kernel(x_ref, sh_ref, sc_ref, w_ref, qn_ref, kvn_ref, wuqt_ref, wuk_ref, wvt_ref, wkt_ref,
                   rcos_ref, rsin_ref, mcos_ref, msin_ref, mtab_ref, rtab_ref,
                   rq_ref, rk_ref, rv_ref, rg_ref, qt_ref, km_ref, vt_ref, gr_ref, gm_ref):
    tm = x_ref.shape[1]
    xf = x_ref[0]
    hb = (_rms(xf) * (1.0 + sc_ref[0]) + sh_ref[0]).astype(BF16)

    def proj(a, b):
        return jnp.dot(hb, w_ref[:, a:b], preferred_element_type=F32)

    lane = lax.broadcasted_iota(jnp.int32, (tm, LANES), 1)
    ret_first = (lane % (RET_QK_DIM // 2)) < (RET_QK_DIM // 4)
    mla_first = (lane % (MLA_ROPE_DIM // 2)) < (MLA_ROPE_DIM // 4)
    rcos, rsin = rcos_ref[...], rsin_ref[...]

    q = proj(_O_RQ, _O_RK)
    for g in range(RET_QK_WIDTH // LANES):
        sl = slice(g * LANES, (g + 1) * LANES)
        rq_ref[0, :, sl] = _rope(q[:, sl], rcos, rsin, ret_first, RET_QK_DIM // 4).astype(BF16)

    kt = lax.dot_general(wkt_ref[...], hb, _NT, preferred_element_type=F32) * RET_SCALE
    nr = RET_QK_DIM // 4
    rcos_r, rsin_r = rtab_ref[0:nr, :], rtab_ref[nr:2 * nr, :]
    rcos_c, rsin_c = rtab_ref[2 * nr:3 * nr, :], rtab_ref[3 * nr:4 * nr, :]
    for h in range(RET_HEADS):
        r0 = h * RET_QK_DIM
        a, b = kt[r0:r0 + nr], kt[r0 + nr:r0 + 2 * nr]
        c, d = kt[r0 + 2 * nr:r0 + 3 * nr], kt[r0 + 3 * nr:r0 + 4 * nr]
        head = jnp.concatenate([a * rcos_r - b * rsin_r, b * rcos_r + a * rsin_r,
                                c * rcos_c - d * rsin_c, d * rcos_c + c * rsin_c], axis=0)
        rk_ref[0, r0:r0 + RET_QK_DIM, :] = head.astype(BF16)
    rv_ref[0] = proj(_O_RV, _O_RG).astype(BF16)
    rg_ref[0] = proj(_O_RG, _O_DQ)

    dq = (_rms(proj(_O_DQ, _O_DKV)) * qn_ref[...]).astype(BF16)
    qt = lax.dot_general(wuqt_ref[...], dq, _NT, preferred_element_type=F32) * _Q_SCALE
    nf = MLA_ROPE_DIM // 4
    cos_r, sin_r = mtab_ref[0:nf, :], mtab_ref[nf:2 * nf, :]
    cos_c, sin_c = mtab_ref[2 * nf:3 * nf, :], mtab_ref[3 * nf:4 * nf, :]
    for h in range(MLA_HEADS):
        r0 = h * LANES + MLA_NOPE_DIM
        a, b = qt[r0:r0 + nf], qt[r0 + nf:r0 + 2 * nf]
        c, d = qt[r0 + 2 * nf:r0 + 3 * nf], qt[r0 + 3 * nf:r0 + 4 * nf]
        head = jnp.concatenate([
            qt[h * LANES:r0],
            a * cos_r - b * sin_r, b * cos_r + a * sin_r,
            c * cos_c - d * sin_c, d * cos_c + c * sin_c,
            qt[r0 + MLA_ROPE_DIM:(h + 1) * LANES]], axis=0)
        qt_ref[0, h * LANES:(h + 1) * LANES, :] = head.astype(BF16)

    dkv = (_rms(proj(_O_DKV, _O_KR)) * kvn_ref[...]).astype(BF16)
    kn = jnp.dot(dkv, wuk_ref[...], preferred_element_type=F32)
    kr = _rope(proj(_O_KR, _O_GR), mcos_ref[...], msin_ref[...], mla_first, MLA_ROPE_DIM // 4)
    for h in range(MLA_HEADS):
        sl = slice(h * LANES, (h + 1) * LANES)
        km_ref[0, :, sl] = (kn[:, sl] + kr).astype(BF16)
    vt = lax.dot_general(wvt_ref[...], dkv, _NT, preferred_element_type=F32)
    pad = _VT_ROWS - MLA_V_DIM
    ones_row = (lax.broadcasted_iota(jnp.int32, (pad, tm), 0) == 0).astype(BF16)
    for h in range(MLA_HEADS):
        vt_ref[0, h * _VT_ROWS:h * _VT_ROWS + MLA_V_DIM, :] = vt[h * MLA_V_DIM:(h + 1) * MLA_V_DIM].astype(BF16)
        vt_ref[0, h * _VT_ROWS + MLA_V_DIM:(h + 1) * _VT_ROWS, :] = ones_row

    gr_ref[0] = proj(_O_GR, _O_GM)
    gm_ref[0] = proj(_O_GM, _IN_W)


def _inproj_call(x, mod, k0, w, q_norm, kv_norm, wuqt, wuk, wvt, wkt, tables):
    bsz, seq, d = x.shape
    tm = min(256, seq)
    row = lambda w_: pl.BlockSpec((1, tm, w_), lambda b, i: (b, i, 0))
    col = lambda r_: pl.BlockSpec((1, r_, tm), lambda b, i: (b, 0, i))
    tab = pl.BlockSpec((tm, LANES), lambda b, i: (i, 0))
    ttab = lambda r_: pl.BlockSpec((r_, tm), lambda b, i: (0, i))
    n_q, n_vt = MLA_HEADS * LANES, MLA_HEADS * _VT_ROWS
    shapes = [(seq, RET_QK_WIDTH), (RET_QK_WIDTH, seq), (seq, RET_V_WIDTH), (seq, RET_V_WIDTH),
              (n_q, seq), (seq, n_q), (n_vt, seq), (seq, RET_V_WIDTH), (seq, RET_V_WIDTH)]
    dtypes = (BF16, BF16, BF16, F32, BF16, BF16, BF16, F32, F32)
    out_specs = [col(s[0]) if i in (1, 4, 6) else row(s[1]) for i, s in enumerate(shapes)]
    return pl.pallas_call(
        _inproj_kernel,
        grid=(bsz, seq // tm),
        in_specs=[
            row(d), _mod_spec(d, k0), _mod_spec(d, k0 + 1),
            _const_spec(w.shape), _const_spec((1, MLA_Q_RANK)), _const_spec((1, MLA_KV_RANK)),
            _const_spec(wuqt.shape), _const_spec(wuk.shape), _const_spec(wvt.shape), _const_spec(wkt.shape),
            tab, tab, tab, tab, ttab(MLA_ROPE_DIM), ttab(RET_QK_DIM),
        ],
        out_specs=out_specs,
        out_shape=[jax.ShapeDtypeStruct((bsz,) + s, dt) for s, dt in zip(shapes, dtypes)],
        compiler_params=_cparams(("parallel", "parallel")),
        name="inproj",
    )(x, mod, mod, w, q_norm.reshape(1, -1), kv_norm.reshape(1, -1), wuqt, wuk, wvt, wkt, *tables)


def _ret_state_kernel(kf_ref, vf_ref, kb_ref, vb_ref, decf_ref, decb_ref, sf0_ref, sb0_ref,
                      sfs_ref, sbs_ref, sf_ref, sb_ref, dm_ref, qs_ref,
                      st_f, st_b, ks_f, ks_b, cd_f, cd_b, *, group):
    b, s = pl.program_id(0), pl.program_id(1)
    c, dk, dv = RET_CHUNK, RET_QK_DIM, RET_V_DIM

    @pl.when(s == 0)
    def _():
        st_f[...] = sf0_ref[0]
        st_b[...] = sb0_ref[0]
        j = lax.broadcasted_iota(jnp.int32, (1, c), 1).astype(F32)
        for h in range(RET_HEADS):
            lgf = -jnp.exp(decf_ref[h])
            lgb = -jnp.exp(decb_ref[h])
            ks_f[h] = jnp.exp(lgf * (c - 1.0 - j))
            ks_b[h] = jnp.exp(lgb * j)
            cd_f[h] = jnp.broadcast_to(jnp.exp(lgf * c), (1, dv))
            cd_b[h] = jnp.broadcast_to(jnp.exp(lgb * c), (1, dv))

    @pl.when((b == 0) & (s == 0))
    def _():
        ii = lax.broadcasted_iota(jnp.int32, (c, c), 0).astype(F32)
        jj = lax.broadcasted_iota(jnp.int32, (c, c), 1).astype(F32)
        diff = ii - jj
        row = lax.broadcasted_iota(jnp.int32, (c, 2 * dk), 0).astype(F32)
        fwd_half = lax.broadcasted_iota(jnp.int32, (c, 2 * dk), 1) < dk
        for h in range(RET_HEADS):
            lgf = -jnp.exp(decf_ref[h])
            lgb = -jnp.exp(decb_ref[h])
            dm_ref[h] = (jnp.where(diff >= 0, jnp.exp(lgf * jnp.maximum(diff, 0.0)), 0.0)
                         + jnp.where(diff <= 0, jnp.exp(lgb * jnp.maximum(-diff, 0.0)), 0.0))
            qs_ref[h] = jnp.exp(jnp.where(fwd_half, lgf * (row + 1.0), lgb * (c - row)))

    kv = {}
    for g in range(group):
        gb = group - 1 - g
        for h in range(RET_HEADS):
            kf = (kf_ref[0, h * dk:(h + 1) * dk, g * c:(g + 1) * c].astype(F32) * ks_f[h]).astype(BF16)
            kv["f", g, h] = jnp.dot(kf, vf_ref[0, g * c:(g + 1) * c, h * dv:(h + 1) * dv],
                                    preferred_element_type=F32)
            kb = (kb_ref[0, h * dk:(h + 1) * dk, gb * c:(gb + 1) * c].astype(F32) * ks_b[h]).astype(BF16)
            kv["b", g, h] = jnp.dot(kb, vb_ref[0, gb * c:(gb + 1) * c, h * dv:(h + 1) * dv],
                                    preferred_element_type=F32)
    for h in range(RET_HEADS):
        sf, sb = st_f[h], st_b[h]
        for g in range(group):
            gb = group - 1 - g
            sfs_ref[0, g, h * dk:(h + 1) * dk, :] = sf.astype(BF16)
            sf = cd_f[h] * sf + kv["f", g, h]
            sbs_ref[0, gb, h * dk:(h + 1) * dk, :] = sb.astype(BF16)
            sb = cd_b[h] * sb + kv["b", g, h]
        st_f[h], st_b[h] = sf, sb

    @pl.when(s == pl.num_programs(1) - 1)
    def _():
        sf_ref[0] = st_f[...]
        sb_ref[0] = st_b[...]


def _ret_state_call(rkt, rv, dec_f, dec_b, s0_f, s0_b):
    bsz, seq, _ = rv.shape
    c = RET_CHUNK
    nc = seq // c
    group = min(4, nc)
    ns = nc // group
    h, dk, dv = RET_HEADS, RET_QK_DIM, RET_V_DIM
    st_spec = pl.BlockSpec((1, h, dk, dv), lambda b, s: (b, 0, 0, 0))
    dec_spec = pl.BlockSpec((h, 1, 1), lambda b, s: (0, 0, 0))
    tab = lambda w_: pl.BlockSpec((h, c, w_), lambda b, s: (0, 0, 0))
    return pl.pallas_call(
        functools.partial(_ret_state_kernel, group=group),
        grid=(bsz, ns),
        in_specs=[
            pl.BlockSpec((1, h * dk, group * c), lambda b, s: (b, 0, s)),
            pl.BlockSpec((1, group * c, h * dv), lambda b, s: (b, s, 0)),
            pl.BlockSpec((1, h * dk, group * c), lambda b, s: (b, 0, ns - 1 - s)),
            pl.BlockSpec((1, group * c, h * dv), lambda b, s: (b, ns - 1 - s, 0)),
            dec_spec, dec_spec, st_spec, st_spec],
        out_specs=[
            pl.BlockSpec((1, group, h * dk, dv), lambda b, s: (b, s, 0, 0)),
            pl.BlockSpec((1, group, h * dk, dv), lambda b, s: (b, ns - 1 - s, 0, 0)),
            st_spec, st_spec, tab(c), tab(2 * dk)],
        out_shape=[jax.ShapeDtypeStruct((bsz, nc, h * dk, dv), BF16)] * 2
        + [jax.ShapeDtypeStruct((bsz, h, dk, dv), F32)] * 2
        + [jax.ShapeDtypeStruct((h, c, c), F32), jax.ShapeDtypeStruct((h, c, 2 * dk), F32)],
        scratch_shapes=[
            pltpu.VMEM((h, dk, dv), F32), pltpu.VMEM((h, dk, dv), F32),
            pltpu.VMEM((h, 1, c), F32), pltpu.VMEM((h, 1, c), F32),
            pltpu.VMEM((h, 1, dv), F32), pltpu.VMEM((h, 1, dv), F32),
        ],
        compiler_params=_cparams(("arbitrary", "arbitrary")),
        name="ret_state",
    )(rkt, rv, rkt, rv, dec_f.reshape(h, 1, 1), dec_b.reshape(h, 1, 1), s0_f, s0_b)


def _attn_kernel(qt_ref, k_ref, vt_ref, o_ref, *, tk, depth):
    tq = qt_ref.shape[2]
    nk = k_ref.shape[1] // tk
    heads = range(2)
    qt = [qt_ref[0, hd * LANES:(hd + 1) * LANES, :] for hd in heads]

    def scores(hd, j):
        k = k_ref[0, j * tk:(j + 1) * tk, hd * LANES:(hd + 1) * LANES]
        return jnp.dot(k, qt[hd], preferred_element_type=F32)

    m = [jnp.full((1, tq), -jnp.inf, F32) for _ in heads]
    acc = [jnp.zeros((_VT_ROWS, tq), F32) for _ in heads]
    pending = {}
    for j in range(-depth, nk):
        if j + depth < nk:
            for hd in heads:
                pending[hd, j + depth] = scores(hd, j + depth)
        if j < 0:
            continue
        for hd in heads:
            st = pending.pop((hd, j))
            vt = vt_ref[0, hd * _VT_ROWS:(hd + 1) * _VT_ROWS, j * tk:(j + 1) * tk]
            m_new = jnp.maximum(m[hd], jnp.max(st, axis=0, keepdims=True))
            alpha = jnp.exp2(m[hd] - m_new)
            pt = jnp.exp2(st - m_new).astype(BF16)
            acc[hd] = alpha * acc[hd] + jnp.dot(vt, pt, preferred_element_type=F32)
            m[hd] = m_new
    outs = [a[:MLA_V_DIM] / a[MLA_V_DIM:MLA_V_DIM + 1] for a in acc]
    o_ref[0] = jnp.concatenate(outs, axis=0).T.astype(o_ref.dtype)


def _attn_call(qt, km, vt):
    bsz, _, seq = qt.shape
    lk = km.shape[1]
    tq = min(256, seq)
    tk = 256
    return pl.pallas_call(
        functools.partial(_attn_kernel, tk=tk, depth=2),
        grid=(bsz, MLA_HEADS // 2, seq // tq),
        in_specs=[
            pl.BlockSpec((1, 2 * LANES, tq), lambda b, h, i: (b, h, i)),
            pl.BlockSpec((1, lk, 2 * LANES), lambda b, h, i: (b, 0, h)),
            pl.BlockSpec((1, 2 * _VT_ROWS, lk), lambda b, h, i: (b, h, 0)),
        ],
        out_specs=pl.BlockSpec((1, tq, LANES), lambda b, h, i: (b, i, h)),
        out_shape=jax.ShapeDtypeStruct((bsz, seq, MLA_OUT_WIDTH), BF16),
        compiler_params=_cparams(("parallel", "parallel", "parallel")),
        name="attention",
    )(qt, km, vt)


def _merge_kernel(x_ref, rq_ref, rkt_ref, rv_ref, sfs_ref, sbs_ref, dm_ref, qs_ref,
                  rg_ref, at_ref, gr_ref, gm_ref, gn_ref, g2_ref,
                  wr_ref, wm_ref, wo_ref, o_ref, a_scr):
    c, dk, dv = RET_CHUNK, RET_QK_DIM, RET_V_DIM
    units = [(ci, h) for ci in range(x_ref.shape[1] // c) for h in range(RET_HEADS)]
    lane = lax.broadcasted_iota(jnp.int32, (c, 2 * dk), 1)
    low = lane < dk

    def q_pair(ci, h):
        g = h // 2
        return rq_ref[0, ci * c:(ci + 1) * c, g * 2 * dk:(g + 1) * 2 * dk]

    scores = {}
    for ci, h in units:
        mine = low if h % 2 == 0 else jnp.logical_not(low)
        qm = jnp.where(mine, q_pair(ci, h), jnp.zeros((), BF16))
        g = h // 2
        kt = rkt_ref[0, g * 2 * dk:(g + 1) * 2 * dk, ci * c:(ci + 1) * c]
        scores[ci, h] = jnp.dot(qm, kt, preferred_element_type=F32)
    outs = {}
    for ci, h in units:
        p = (scores.pop((ci, h)) * dm_ref[h]).astype(BF16)
        qf = q_pair(ci, h).astype(F32)
        qr = pltpu.roll(qf, dk, 1)
        both = jnp.where(low, qf, qr) if h % 2 == 0 else jnp.where(low, qr, qf)
        lhs = jnp.concatenate([p, (both * qs_ref[h]).astype(BF16)], axis=1)
        rhs = jnp.concatenate([rv_ref[0, ci * c:(ci + 1) * c, h * dv:(h + 1) * dv],
                               sfs_ref[0, ci, h * dk:(h + 1) * dk, :],
                               sbs_ref[0, ci, h * dk:(h + 1) * dk, :]], axis=0)
        outs[ci, h] = jnp.dot(lhs, rhs, preferred_element_type=F32)
    for ci, h in units:
        rows, sl = slice(ci * c, (ci + 1) * c), slice(h * dv, (h + 1) * dv)
        o = outs.pop((ci, h))
        d = o - jnp.mean(o, axis=-1, keepdims=True)
        nrm = d * lax.rsqrt(jnp.mean(d * d, axis=-1, keepdims=True) + EPS) * gn_ref[:, sl]
        a_scr[rows, sl] = (_silu(rg_ref[0, rows, sl]) * nrm).astype(BF16)
    ret_l = jnp.dot(a_scr[...], wr_ref[...], preferred_element_type=F32)
    mla_l = jnp.dot(at_ref[0], wm_ref[...], preferred_element_type=F32)
    mrg = jax.nn.sigmoid(gr_ref[0]) * ret_l + jax.nn.sigmoid(gm_ref[0]) * mla_l
    out = jnp.dot(mrg.astype(BF16), wo_ref[...], preferred_element_type=F32)
    o_ref[0] = x_ref[0] + g2_ref[0] * out


def _merge_call(x, rq, rkt, rv, sfs, sbs, dm, qs, rg, att, gr, gm, gn, mod, k_gate, wr, wm, wo):
    bsz, seq, d = x.shape
    tm = min(256, seq)
    nch = tm // RET_CHUNK
    row = lambda w_: pl.BlockSpec((1, tm, w_), lambda b, i: (b, i, 0))
    state = pl.BlockSpec((1, nch, RET_QK_WIDTH, RET_V_DIM), lambda b, i: (b, i, 0, 0))
    return pl.pallas_call(
        _merge_kernel,
        grid=(bsz, seq // tm),
        in_specs=[row(d), row(RET_QK_WIDTH), pl.BlockSpec((1, RET_QK_WIDTH, tm), lambda b, i: (b, 0, i)),
                  row(RET_V_WIDTH), state, state, _const_spec(dm.shape), _const_spec(qs.shape),
                  row(RET_V_WIDTH), row(MLA_OUT_WIDTH),
                  row(RET_V_WIDTH), row(RET_V_WIDTH), _const_spec((1, RET_V_WIDTH)), _mod_spec(d, k_gate),
                  _const_spec(wr.shape), _const_spec(wm.shape), _const_spec(wo.shape)],
        out_specs=row(d),
        out_shape=jax.ShapeDtypeStruct(x.shape, F32),
        scratch_shapes=[pltpu.VMEM((tm, RET_V_WIDTH), BF16)],
        compiler_params=_cparams(("parallel", "parallel")),
        name="merge",
    )(x, rq, rkt, rv, sfs, sbs, dm, qs, rg, att, gr, gm, gn.reshape(1, -1), mod, wr, wm, wo)


def _rope_tables(length, n_freq):
    rows = length // GRID_W
    row = jnp.repeat(jnp.arange(rows, dtype=F32), GRID_W)
    col = jnp.tile(jnp.arange(GRID_W, dtype=F32), rows)
    inv_freq = jnp.power(ROPE_BASE, -jnp.arange(n_freq, dtype=F32) / n_freq)
    ang_r = row[:, None] * inv_freq[None, :]
    ang_c = col[:, None] * inv_freq[None, :]
    return jnp.cos(ang_r), jnp.sin(ang_r), jnp.cos(ang_c), jnp.sin(ang_c)


def _pair_tables(t):
    cos_r, sin_r, cos_c, sin_c = t
    return (jnp.concatenate([cos_r, cos_r, cos_c, cos_c], axis=-1),
            jnp.concatenate([-sin_r, sin_r, -sin_c, sin_c], axis=-1))


def _tables(length, rope):
    nr, nm = RET_QK_DIM // 4, MLA_ROPE_DIM // 4
    if rope:
        rt, mt = _rope_tables(length, nr), _rope_tables(length, nm)
    else:
        one, zero = jnp.ones((length, 1), F32), jnp.zeros((length, 1), F32)
        rt = tuple(jnp.tile(v, (1, nr)) for v in (one, zero, one, zero))
        mt = tuple(jnp.tile(v, (1, nm)) for v in (one, zero, one, zero))
    rc, rs = _pair_tables(rt)
    mc, ms = _pair_tables(mt)
    reps = LANES // RET_QK_DIM
    tail = LANES - MLA_NOPE_DIM - MLA_ROPE_DIM
    mcos = jnp.concatenate([jnp.ones((length, MLA_NOPE_DIM), F32), mc, jnp.ones((length, tail), F32)], axis=-1)
    msin = jnp.concatenate([jnp.zeros((length, MLA_NOPE_DIM), F32), ms, jnp.zeros((length, tail), F32)], axis=-1)
    mtab = jnp.concatenate(mt, axis=-1).T
    rtab = jnp.concatenate(rt, axis=-1).T
    return jnp.tile(rc, (1, reps)), jnp.tile(rs, (1, reps)), mcos, msin, mtab, rtab


def _prep_w_in(w_in):
    d = w_in.shape[0]
    o_kr = 2 * RET_QK_WIDTH + 2 * RET_V_WIDTH + MLA_Q_RANK + MLA_KV_RANK
    kr = jnp.concatenate([jnp.zeros((d, MLA_NOPE_DIM), F32), w_in[:, o_kr:o_kr + MLA_ROPE_DIM],
                          jnp.zeros((d, LANES - MLA_NOPE_DIM - MLA_ROPE_DIM), F32)], axis=1)
    return jnp.concatenate([w_in[:, :o_kr], kr, w_in[:, o_kr + MLA_ROPE_DIM:]], axis=1).astype(BF16)


def _prep_w_uq(w_uq):
    r = w_uq.shape[0]
    w = w_uq.reshape(r, MLA_HEADS, MLA_QK_DIM)
    w = jnp.pad(w, ((0, 0), (0, 0), (0, LANES - MLA_QK_DIM)))
    return w.reshape(r, MLA_HEADS * LANES).T.astype(BF16)


def _prep_w_ukv(w_ukv):
    r = w_ukv.shape[0]
    w = w_ukv.reshape(r, MLA_HEADS, MLA_NOPE_DIM + MLA_V_DIM)
    wk = jnp.pad(w[..., :MLA_NOPE_DIM], ((0, 0), (0, 0), (0, LANES - MLA_NOPE_DIM)))
    wv = w[..., MLA_NOPE_DIM:]
    return wk.reshape(r, MLA_HEADS * LANES).astype(BF16), wv.reshape(r, MLA_OUT_WIDTH).T.astype(BF16)


def kernel(x, c, ctx, c_ctx, w_ada, b_ada, ffn1_w1, ffn1_w3, ffn1_w2, ffn2_w1, ffn2_w3, ffn2_w2, w_in,
           ret_decay_fwd, ret_decay_bwd, ret_gn, mla_q_norm, mla_kv_norm, w_uq, w_ukv, w_ret_out,
           w_mla_out, w_o, final_norm):
    bsz, seq, d = x.shape
    lc = ctx.shape[1]
    depth = w_ada.shape[0]

    rows = -(-(bsz + 1) // 8) * 8
    cc = jnp.zeros((rows, d), F32).at[:bsz].set(c).at[bsz].set(c_ctx)
    mod_all = _mod_call(cc, w_ada, b_ada)

    tab_lat = _tables(seq, True)
    tab_ctx = _tables(lc, False)
    zero_state = jnp.zeros((bsz, RET_HEADS, RET_QK_DIM, RET_V_DIM), F32)

    xc = ctx
    for l in range(depth):
        last = l == depth - 1
        mod = mod_all[l, :bsz][:, None, :]
        mod_c = jnp.broadcast_to(mod_all[l, bsz][None, None, :], (bsz, 1, N_MOD * d))
        f1 = (ffn1_w1[l].astype(BF16), ffn1_w3[l].astype(BF16), ffn1_w2[l].astype(BF16))
        f2 = (ffn2_w1[l].astype(BF16), ffn2_w3[l].astype(BF16), ffn2_w2[l].astype(BF16))
        w_in_l = _prep_w_in(w_in[l])
        wuqt_l = _prep_w_uq(w_uq[l])
        wuk_l, wvt_l = _prep_w_ukv(w_ukv[l])
        wr, wm, wo = w_ret_out[l].astype(BF16), w_mla_out[l].astype(BF16), w_o[l].astype(BF16)

        x = _ffn_call(x, mod, 0, *f1)
        xc = _ffn_call(xc, mod_c, 0, *f1)

        wkt_l = w_in[l][:, RET_QK_WIDTH:2 * RET_QK_WIDTH].T.astype(BF16)
        mla_w = (w_in_l, mla_q_norm[l], mla_kv_norm[l], wuqt_l, wuk_l, wvt_l, wkt_l)
        rq, rkt, rv, rg, qt, km, vt, gr, gm = _inproj_call(x, mod, 3, *mla_w, tab_lat)
        crq, crkt, crv, crg, cqt, ckm, cvt, cgr, cgm = _inproj_call(xc, mod_c, 3, *mla_w, tab_ctx)

        dec = (ret_decay_fwd[l], ret_decay_bwd[l])
        csf, csb, sc_f, sc_b, dm, qs = _ret_state_call(crkt, crv, *dec, zero_state, zero_state)
        sfs, sbs, _, _, _, _ = _ret_state_call(rkt, rv, *dec, sc_f, sc_b)

        att = _attn_call(qt, jnp.concatenate([km, ckm], axis=1), jnp.concatenate([vt, cvt], axis=2))
        x = _merge_call(x, rq, rkt, rv, sfs, sbs, dm, qs, rg, att, gr, gm, ret_gn[l], mod, 5, wr, wm, wo)
        x = _ffn_call(x, mod, 6, *f2, final_gain=final_norm if last else None)
        if not last:
            att_c = _attn_call(cqt, ckm, cvt)
            xc = _merge_call(xc, crq, crkt, crv, csf, csb, dm, qs, crg, att_c, cgr, cgm, ret_gn[l],
                             mod_c, 5, wr, wm, wo)
            xc = _ffn_call(xc, mod_c, 6, *f2)
    return x
```

```python
import functools

import jax
import jax.numpy as jnp
from jax import lax
from jax.experimental import pallas as pl
from jax.experimental.pallas import tpu as pltpu

F32 = jnp.float32
BF16 = jnp.bfloat16

GRID_W = 64
EPS = 1e-6
ROPE_BASE = 10000.0
N_MOD = 9

RET_HEADS = 8
RET_QK_DIM = 64
RET_V_DIM = 128
RET_CHUNK = 128
RET_QK_WIDTH = RET_HEADS * RET_QK_DIM
RET_V_WIDTH = RET_HEADS * RET_V_DIM
RET_SCALE = RET_QK_DIM ** -0.5

MLA_HEADS = 8
MLA_Q_RANK = 384
MLA_KV_RANK = 256
MLA_NOPE_DIM = 64
MLA_ROPE_DIM = 32
MLA_V_DIM = 64
MLA_QK_DIM = MLA_NOPE_DIM + MLA_ROPE_DIM
MLA_OUT_WIDTH = MLA_HEADS * MLA_V_DIM
MLA_SCALE = MLA_QK_DIM ** -0.5

LANES = 128
MLA_HEAD_PAD = LANES
VMEM_LIMIT = 56 * 1024 * 1024
BF16_SUBLANES = 16
_VT_ROWS = MLA_V_DIM + BF16_SUBLANES
_Q_SCALE = MLA_SCALE * 1.4426950408889634
_NT = (((1,), (1,)), ((), ()))

_O_RQ = 0
_O_RK = _O_RQ + RET_QK_WIDTH
_O_RV = _O_RK + RET_QK_WIDTH
_O_RG = _O_RV + RET_V_WIDTH
_O_DQ = _O_RG + RET_V_WIDTH
_O_DKV = _O_DQ + MLA_Q_RANK
_O_KR = _O_DKV + MLA_KV_RANK
_O_GR = _O_KR + LANES
_O_GM = _O_GR + RET_V_WIDTH
_IN_W = _O_GM + RET_V_WIDTH


def _cparams(sem, vmem=VMEM_LIMIT):
    return pltpu.CompilerParams(dimension_semantics=sem, vmem_limit_bytes=vmem)


def _const_spec(shape):
    n = len(shape)
    return pl.BlockSpec(shape, lambda *_: (0,) * n, pipeline_mode=pl.Buffered(1))


def _rms(xf):
    return xf * lax.rsqrt(jnp.mean(xf * xf, axis=-1, keepdims=True) + EPS)


def _silu(a):
    return a * jax.nn.sigmoid(a)


def _mod_kernel(c_ref, w_ref, b_ref, o_ref):
    s = _silu(c_ref[...])
    o_ref[0] = jnp.dot(s, w_ref[0], preferred_element_type=F32,
                       precision=lax.Precision.HIGHEST) + b_ref[0]


def _mod_call(cc, w_ada, b_ada):
    depth, d, nd = w_ada.shape
    rows = cc.shape[0]
    tn = 1024
    return pl.pallas_call(
        _mod_kernel,
        grid=(depth, nd // tn),
        in_specs=[
            pl.BlockSpec((rows, d), lambda l, j: (0, 0)),
            pl.BlockSpec((1, d, tn), lambda l, j: (l, 0, j)),
            pl.BlockSpec((1, 1, tn), lambda l, j: (l, 0, j)),
        ],
        out_specs=pl.BlockSpec((1, rows, tn), lambda l, j: (l, 0, j)),
        out_shape=jax.ShapeDtypeStruct((depth, rows, nd), F32),
        compiler_params=_cparams(("arbitrary", "arbitrary")),
        name="mod",
    )(cc, w_ada, b_ada.reshape(depth, 1, nd))


def _ffn_kernel(x_ref, sh_ref, sc_ref, g_ref, w1_ref, w3_ref, w2_ref, *rest, final):
    if final:
        fn_ref, o_ref = rest
    else:
        (o_ref,) = rest
    xf = x_ref[0]
    h = _rms(xf) * (1.0 + sc_ref[0]) + sh_ref[0]
    hb = h.astype(BF16)
    a = jnp.dot(hb, w1_ref[...], preferred_element_type=F32)
    b = jnp.dot(hb, w3_ref[...], preferred_element_type=F32)
    u = (_silu(a) * b).astype(BF16)
    y = jnp.dot(u, w2_ref[...], preferred_element_type=F32)
    xn = xf + (0.5 * g_ref[0]) * y
    if final:
        xn = _rms(xn) * fn_ref[...]
    o_ref[0] = xn


def _mod_spec(d, k):
    return pl.BlockSpec((1, 1, d), lambda b, i: (b, 0, k))


def _ffn_call(x, mod, k0, w1, w3, w2, final_gain=None):
    bsz, seq, d = x.shape
    dff = w1.shape[1]
    tm = min(512, seq)
    final = final_gain is not None
    in_specs = [
        pl.BlockSpec((1, tm, d), lambda b, i: (b, i, 0)),
        _mod_spec(d, k0), _mod_spec(d, k0 + 1), _mod_spec(d, k0 + 2),
        _const_spec((d, dff)), _const_spec((d, dff)), _const_spec((dff, d)),
    ]
    args = [x, mod, mod, mod, w1, w3, w2]
    if final:
        in_specs.append(_const_spec((1, d)))
        args.append(final_gain.reshape(1, d))
    return pl.pallas_call(
        functools.partial(_ffn_kernel, final=final),
        grid=(bsz, seq // tm),
        in_specs=in_specs,
        out_specs=pl.BlockSpec((1, tm, d), lambda b, i: (b, i, 0)),
        out_shape=jax.ShapeDtypeStruct(x.shape, F32),
        compiler_params=_cparams(("parallel", "parallel")),
        name="ffn_final" if final else "ffn",
    )(*args)


def _rope(y, cos, sin, first_half, off):
    partner = jnp.where(first_half, pltpu.roll(y, LANES - off, 1), pltpu.roll(y, off, 1))
    return y * cos + partner * sin


def _inproj_kernel(x_ref, sh_ref, sc_ref, w_ref, qn_ref, kvn_ref, wuqt_ref, wuk_ref, wvt_ref, wkt_ref,
                   rcos_ref, rsin_ref, mcos_ref, msin_ref, mtab_ref, rtab_ref,
                   rq_ref, rk_ref, rv_ref, rg_ref, qt_ref, km_ref, vt_ref, gr_ref, gm_ref):
    tm = x_ref.shape[1]
    xf = x_ref[0]
    hb = (_rms(xf) * (1.0 + sc_ref[0]) + sh_ref[0]).astype(BF16)

    def proj(a, b):
        return jnp.dot(hb, w_ref[:, a:b], preferred_element_type=F32)

    lane = lax.broadcasted_iota(jnp.int32, (tm, LANES), 1)
    ret_first = (lane % (RET_QK_DIM // 2)) < (RET_QK_DIM // 4)
    mla_first = (lane % (MLA_ROPE_DIM // 2)) < (MLA_ROPE_DIM // 4)
    rcos, rsin = rcos_ref[...], rsin_ref[...]

    dq_raw = proj(_O_DQ, _O_DKV)
    dkv_raw = proj(_O_DKV, _O_KR)
    kr_raw = proj(_O_KR, _O_GR)
    q = proj(_O_RQ, _O_RK)
    kt = lax.dot_general(wkt_ref[...], hb, _NT, preferred_element_type=F32) * RET_SCALE
    rv_ref[0] = proj(_O_RV, _O_RG).astype(BF16)
    rg_ref[0] = proj(_O_RG, _O_DQ).astype(BF16)
    dq = (_rms(dq_raw) * qn_ref[...]).astype(BF16)
    dkv = (_rms(dkv_raw) * kvn_ref[...]).astype(BF16)
    qt = lax.dot_general(wuqt_ref[...], dq, _NT, preferred_element_type=F32) * _Q_SCALE
    kn = jnp.dot(dkv, wuk_ref[...], preferred_element_type=F32)
    vt = lax.dot_general(wvt_ref[...], dkv, _NT, preferred_element_type=F32)
    gr_ref[0] = proj(_O_GR, _O_GM).astype(BF16)
    gm_ref[0] = proj(_O_GM, _IN_W).astype(BF16)

    for g in range(RET_QK_WIDTH // LANES):
        sl = slice(g * LANES, (g + 1) * LANES)
        rq_ref[0, :, sl] = _rope(q[:, sl], rcos, rsin, ret_first, RET_QK_DIM // 4).astype(BF16)

    nr = RET_QK_DIM // 4
    rcos_r, rsin_r = rtab_ref[0:nr, :], rtab_ref[nr:2 * nr, :]
    rcos_c, rsin_c = rtab_ref[2 * nr:3 * nr, :], rtab_ref[3 * nr:4 * nr, :]
    for h in range(RET_HEADS):
        r0 = h * RET_QK_DIM
        a, b = kt[r0:r0 + nr], kt[r0 + nr:r0 + 2 * nr]
        c, d = kt[r0 + 2 * nr:r0 + 3 * nr], kt[r0 + 3 * nr:r0 + 4 * nr]
        head = jnp.concatenate([a * rcos_r - b * rsin_r, b * rcos_r + a * rsin_r,
                                c * rcos_c - d * rsin_c, d * rcos_c + c * rsin_c], axis=0)
        rk_ref[0, r0:r0 + RET_QK_DIM, :] = head.astype(BF16)

    nf = MLA_ROPE_DIM // 4
    cos_r, sin_r = mtab_ref[0:nf, :], mtab_ref[nf:2 * nf, :]
    cos_c, sin_c = mtab_ref[2 * nf:3 * nf, :], mtab_ref[3 * nf:4 * nf, :]
    for h in range(MLA_HEADS):
        r0 = h * LANES + MLA_NOPE_DIM
        a, b = qt[r0:r0 + nf], qt[r0 + nf:r0 + 2 * nf]
        c, d = qt[r0 + 2 * nf:r0 + 3 * nf], qt[r0 + 3 * nf:r0 + 4 * nf]
        head = jnp.concatenate([
            qt[h * LANES:r0],
            a * cos_r - b * sin_r, b * cos_r + a * sin_r,
            c * cos_c - d * sin_c, d * cos_c + c * sin_c,
            qt[r0 + MLA_ROPE_DIM:(h + 1) * LANES]], axis=0)
        qt_ref[0, h * LANES:(h + 1) * LANES, :] = head.astype(BF16)

    kr = _rope(kr_raw, mcos_ref[...], msin_ref[...], mla_first, MLA_ROPE_DIM // 4)
    for h in range(MLA_HEADS):
        sl = slice(h * LANES, (h + 1) * LANES)
        km_ref[0, :, sl] = (kn[:, sl] + kr).astype(BF16)
    pad = _VT_ROWS - MLA_V_DIM
    ones_row = (lax.broadcasted_iota(jnp.int32, (pad, tm), 0) == 0).astype(BF16)
    for h in range(MLA_HEADS):
        vt_ref[0, h * _VT_ROWS:h * _VT_ROWS + MLA_V_DIM, :] = vt[h * MLA_V_DIM:(h + 1) * MLA_V_DIM].astype(BF16)
        vt_ref[0, h * _VT_ROWS + MLA_V_DIM:(h + 1) * _VT_ROWS, :] = ones_row


def _inproj_call(x, mod, k0, w, q_norm, kv_norm, wuqt, wuk, wvt, wkt, tables):
    bsz, seq, d = x.shape
    tm = min(512, seq)
    row = lambda w_: pl.BlockSpec((1, tm, w_), lambda b, i: (b, i, 0))
    col = lambda r_: pl.BlockSpec((1, r_, tm), lambda b, i: (b, 0, i))
    tab = pl.BlockSpec((tm, LANES), lambda b, i: (i, 0))
    ttab = lambda r_: pl.BlockSpec((r_, tm), lambda b, i: (0, i))
    n_q, n_vt = MLA_HEADS * LANES, MLA_HEADS * _VT_ROWS
    shapes = [(seq, RET_QK_WIDTH), (RET_QK_WIDTH, seq), (seq, RET_V_WIDTH), (seq, RET_V_WIDTH),
              (n_q, seq), (seq, n_q), (n_vt, seq), (seq, RET_V_WIDTH), (seq, RET_V_WIDTH)]
    dtypes = (BF16,) * len(shapes)
    out_specs = [col(s[0]) if i in (1, 4, 6) else row(s[1]) for i, s in enumerate(shapes)]
    return pl.pallas_call(
        _inproj_kernel,
        grid=(bsz, seq // tm),
        in_specs=[
            row(d), _mod_spec(d, k0), _mod_spec(d, k0 + 1),
            _const_spec(w.shape), _const_spec((1, MLA_Q_RANK)), _const_spec((1, MLA_KV_RANK)),
            _const_spec(wuqt.shape), _const_spec(wuk.shape), _const_spec(wvt.shape), _const_spec(wkt.shape),
            tab, tab, tab, tab, ttab(MLA_ROPE_DIM), ttab(RET_QK_DIM),
        ],
        out_specs=out_specs,
        out_shape=[jax.ShapeDtypeStruct((bsz,) + s, dt) for s, dt in zip(shapes, dtypes)],
        compiler_params=_cparams(("parallel", "parallel")),
        name="inproj",
    )(x, mod, mod, w, q_norm.reshape(1, -1), kv_norm.reshape(1, -1), wuqt, wuk, wvt, wkt, *tables)


def _ret_state_kernel(kf_ref, vf_ref, kb_ref, vb_ref, decf_ref, decb_ref, sf0_ref, sb0_ref,
                      sfs_ref, sbs_ref, sf_ref, sb_ref, dm_ref, qs_ref,
                      st_f, st_b, ks_f, ks_b, cd_f, cd_b, *, group):
    b, s = pl.program_id(0), pl.program_id(1)
    c, dk, dv = RET_CHUNK, RET_QK_DIM, RET_V_DIM

    @pl.when(s == 0)
    def _():
        st_f[...] = sf0_ref[0]
        st_b[...] = sb0_ref[0]
        j = lax.broadcasted_iota(jnp.int32, (1, c), 1).astype(F32)
        for h in range(RET_HEADS):
            lgf = -jnp.exp(decf_ref[h])
            lgb = -jnp.exp(decb_ref[h])
            ks_f[h] = jnp.exp(lgf * (c - 1.0 - j))
            ks_b[h] = jnp.exp(lgb * j)
            cd_f[h] = jnp.broadcast_to(jnp.exp(lgf * c), (1, dv))
            cd_b[h] = jnp.broadcast_to(jnp.exp(lgb * c), (1, dv))

    @pl.when((b == 0) & (s == 0))
    def _():
        ii = lax.broadcasted_iota(jnp.int32, (c, c), 0).astype(F32)
        jj = lax.broadcasted_iota(jnp.int32, (c, c), 1).astype(F32)
        diff = ii - jj
        row = lax.broadcasted_iota(jnp.int32, (c, 2 * dk), 0).astype(F32)
        fwd_half = lax.broadcasted_iota(jnp.int32, (c, 2 * dk), 1) < dk
        for h in range(RET_HEADS):
            lgf = -jnp.exp(decf_ref[h])
            lgb = -jnp.exp(decb_ref[h])
            dm_ref[h] = (jnp.where(diff >= 0, jnp.exp(lgf * jnp.maximum(diff, 0.0)), 0.0)
                         + jnp.where(diff <= 0, jnp.exp(lgb * jnp.maximum(-diff, 0.0)), 0.0))
            qs_ref[h] = jnp.exp(jnp.where(fwd_half, lgf * (row + 1.0), lgb * (c - row)))

    kv = {}
    for g in range(group):
        gb = group - 1 - g
        for h in range(RET_HEADS):
            kf = (kf_ref[0, h * dk:(h + 1) * dk, g * c:(g + 1) * c].astype(F32) * ks_f[h]).astype(BF16)
            kv["f", g, h] = jnp.dot(kf, vf_ref[0, g * c:(g + 1) * c, h * dv:(h + 1) * dv],
                                    preferred_element_type=F32)
            kb = (kb_ref[0, h * dk:(h + 1) * dk, gb * c:(gb + 1) * c].astype(F32) * ks_b[h]).astype(BF16)
            kv["b", g, h] = jnp.dot(kb, vb_ref[0, gb * c:(gb + 1) * c, h * dv:(h + 1) * dv],
                                    preferred_element_type=F32)
    for h in range(RET_HEADS):
        sf, sb = st_f[h], st_b[h]
        for g in range(group):
            gb = group - 1 - g
            sfs_ref[0, g, h * dk:(h + 1) * dk, :] = sf.astype(BF16)
            sf = cd_f[h] * sf + kv["f", g, h]
            sbs_ref[0, gb, h * dk:(h + 1) * dk, :] = sb.astype(BF16)
            sb = cd_b[h] * sb + kv["b", g, h]
        st_f[h], st_b[h] = sf, sb

    @pl.when(s == pl.num_programs(1) - 1)
    def _():
        sf_ref[0] = st_f[...]
        sb_ref[0] = st_b[...]


def _ret_state_call(rkt, rv, dec_f, dec_b, s0_f, s0_b):
    bsz, seq, _ = rv.shape
    c = RET_CHUNK
    nc = seq // c
    group = min(4, nc)
    ns = nc // group
    h, dk, dv = RET_HEADS, RET_QK_DIM, RET_V_DIM
    st_spec = pl.BlockSpec((1, h, dk, dv), lambda b, s: (b, 0, 0, 0))
    dec_spec = pl.BlockSpec((h, 1, 1), lambda b, s: (0, 0, 0))
    tab = lambda w_: pl.BlockSpec((h, c, w_), lambda b, s: (0, 0, 0))
    return pl.pallas_call(
        functools.partial(_ret_state_kernel, group=group),
        grid=(bsz, ns),
        in_specs=[
            pl.BlockSpec((1, h * dk, group * c), lambda b, s: (b, 0, s)),
            pl.BlockSpec((1, group * c, h * dv), lambda b, s: (b, s, 0)),
            pl.BlockSpec((1, h * dk, group * c), lambda b, s: (b, 0, ns - 1 - s)),
            pl.BlockSpec((1, group * c, h * dv), lambda b, s: (b, ns - 1 - s, 0)),
            dec_spec, dec_spec, st_spec, st_spec],
        out_specs=[
            pl.BlockSpec((1, group, h * dk, dv), lambda b, s: (b, s, 0, 0)),
            pl.BlockSpec((1, group, h * dk, dv), lambda b, s: (b, ns - 1 - s, 0, 0)),
            st_spec, st_spec, tab(c), tab(2 * dk)],
        out_shape=[jax.ShapeDtypeStruct((bsz, nc, h * dk, dv), BF16)] * 2
        + [jax.ShapeDtypeStruct((bsz, h, dk, dv), F32)] * 2
        + [jax.ShapeDtypeStruct((h, c, c), F32), jax.ShapeDtypeStruct((h, c, 2 * dk), F32)],
        scratch_shapes=[
            pltpu.VMEM((h, dk, dv), F32), pltpu.VMEM((h, dk, dv), F32),
            pltpu.VMEM((h, 1, c), F32), pltpu.VMEM((h, 1, c), F32),
            pltpu.VMEM((h, 1, dv), F32), pltpu.VMEM((h, 1, dv), F32),
        ],
        compiler_params=_cparams(("arbitrary", "arbitrary")),
        name="ret_state",
    )(rkt, rv, rkt, rv, dec_f.reshape(h, 1, 1), dec_b.reshape(h, 1, 1), s0_f, s0_b)


def _attn_kernel(qt_ref, *refs, tk, depth):
    o_ref = refs[-1]
    sources = [(refs[i], refs[i + 1]) for i in range(0, len(refs) - 1, 2)]
    chunks = [(k_ref, vt_ref, j) for k_ref, vt_ref in sources for j in range(k_ref.shape[1] // tk)]
    nk = len(chunks)
    tq = qt_ref.shape[2]
    heads = range(qt_ref.shape[1] // LANES)
    qt = [qt_ref[0, hd * LANES:(hd + 1) * LANES, :] for hd in heads]

    def scores(hd, n):
        k_ref, _, j = chunks[n]
        k = k_ref[0, j * tk:(j + 1) * tk, hd * LANES:(hd + 1) * LANES]
        return jnp.dot(k, qt[hd], preferred_element_type=F32)

    m = [jnp.full((1, tq), -jnp.inf, F32) for _ in heads]
    acc = [jnp.zeros((_VT_ROWS, tq), F32) for _ in heads]
    pending = {}
    for n in range(-depth, nk):
        if n + depth < nk:
            for hd in heads:
                pending[hd, n + depth] = scores(hd, n + depth)
        if n < 0:
            continue
        _, vt_ref, j = chunks[n]
        pts, alphas = [], []
        for hd in heads:
            st = pending.pop((hd, n))
            m_new = jnp.maximum(m[hd], jnp.max(st, axis=0, keepdims=True))
            alphas.append(jnp.exp2(m[hd] - m_new))
            pts.append(jnp.exp2(st - m_new).astype(BF16))
            m[hd] = m_new
        for hd in heads:
            vt = vt_ref[0, hd * _VT_ROWS:(hd + 1) * _VT_ROWS, j * tk:(j + 1) * tk]
            acc[hd] = alphas[hd] * acc[hd] + jnp.dot(vt, pts[hd], preferred_element_type=F32)
    outs = [a[:MLA_V_DIM] / a[MLA_V_DIM:MLA_V_DIM + 1] for a in acc]
    o_ref[0] = jnp.concatenate(outs, axis=0).T.astype(o_ref.dtype)


def _attn_call(qt, *kv):
    bsz, _, seq = qt.shape
    tq = min(256, seq)
    tk = 256
    hps = 2
    in_specs = [pl.BlockSpec((1, hps * LANES, tq), lambda b, h, i: (b, h, i))]
    for km, vt in zip(kv[::2], kv[1::2]):
        lk = km.shape[1]
        in_specs += [pl.BlockSpec((1, lk, hps * LANES), lambda b, h, i: (b, 0, h)),
                     pl.BlockSpec((1, hps * _VT_ROWS, lk), lambda b, h, i: (b, h, 0))]
    return pl.pallas_call(
        functools.partial(_attn_kernel, tk=tk, depth=2),
        grid=(bsz, MLA_HEADS // hps, seq // tq),
        in_specs=in_specs,
        out_specs=pl.BlockSpec((1, tq, hps * MLA_V_DIM), lambda b, h, i: (b, i, h)),
        out_shape=jax.ShapeDtypeStruct((bsz, seq, MLA_OUT_WIDTH), BF16),
        compiler_params=_cparams(("parallel", "parallel", "parallel")),
        name="attention",
    )(qt, *kv)


def _merge_kernel(x_ref, rq_ref, rkt_ref, rv_ref, sfs_ref, sbs_ref, dm_ref, qs_ref,
                  rg_ref, at_ref, gr_ref, gm_ref, gn_ref, g2_ref,
                  wr_ref, wm_ref, wo_ref, o_ref, a_scr):
    c, dk, dv = RET_CHUNK, RET_QK_DIM, RET_V_DIM
    units = [(ci, h) for ci in range(x_ref.shape[1] // c) for h in range(RET_HEADS)]
    lane = lax.broadcasted_iota(jnp.int32, (c, 2 * dk), 1)
    low = lane < dk

    def q_pair(ci, h):
        g = h // 2
        return rq_ref[0, ci * c:(ci + 1) * c, g * 2 * dk:(g + 1) * 2 * dk]

    scores = {}
    for ci, h in units:
        mine = low if h % 2 == 0 else jnp.logical_not(low)
        qm = jnp.where(mine, q_pair(ci, h), jnp.zeros((), BF16))
        g = h // 2
        kt = rkt_ref[0, g * 2 * dk:(g + 1) * 2 * dk, ci * c:(ci + 1) * c]
        scores[ci, h] = jnp.dot(qm, kt, preferred_element_type=F32)
    outs = {}
    for ci, h in units:
        p = (scores.pop((ci, h)) * dm_ref[h]).astype(BF16)
        qf = q_pair(ci, h).astype(F32)
        qr = pltpu.roll(qf, dk, 1)
        both = jnp.where(low, qf, qr) if h % 2 == 0 else jnp.where(low, qr, qf)
        lhs = jnp.concatenate([p, (both * qs_ref[h]).astype(BF16)], axis=1)
        rhs = jnp.concatenate([rv_ref[0, ci * c:(ci + 1) * c, h * dv:(h + 1) * dv],
                               sfs_ref[0, ci, h * dk:(h + 1) * dk, :],
                               sbs_ref[0, ci, h * dk:(h + 1) * dk, :]], axis=0)
        outs[ci, h] = jnp.dot(lhs, rhs, preferred_element_type=F32)
    mla_l = jnp.dot(at_ref[0], wm_ref[...], preferred_element_type=F32)
    for ci, h in units:
        rows, sl = slice(ci * c, (ci + 1) * c), slice(h * dv, (h + 1) * dv)
        o = outs.pop((ci, h))
        d = o - jnp.mean(o, axis=-1, keepdims=True)
        nrm = d * lax.rsqrt(jnp.mean(d * d, axis=-1, keepdims=True) + EPS) * gn_ref[:, sl]
        a_scr[rows, sl] = (_silu(rg_ref[0, rows, sl].astype(F32)) * nrm).astype(BF16)
    ret_l = jnp.dot(a_scr[...], wr_ref[...], preferred_element_type=F32)
    mrg = (jax.nn.sigmoid(gr_ref[0].astype(F32)) * ret_l
           + jax.nn.sigmoid(gm_ref[0].astype(F32)) * mla_l)
    out = jnp.dot(mrg.astype(BF16), wo_ref[...], preferred_element_type=F32)
    o_ref[0] = x_ref[0] + g2_ref[0] * out


def _merge_call(x, rq, rkt, rv, sfs, sbs, dm, qs, rg, att, gr, gm, gn, mod, k_gate, wr, wm, wo):
    bsz, seq, d = x.shape
    tm = min(512, seq)
    nch = tm // RET_CHUNK
    row = lambda w_: pl.BlockSpec((1, tm, w_), lambda b, i: (b, i, 0))
    state = pl.BlockSpec((1, nch, RET_QK_WIDTH, RET_V_DIM), lambda b, i: (b, i, 0, 0))
    return pl.pallas_call(
        _merge_kernel,
        grid=(bsz, seq // tm),
        in_specs=[row(d), row(RET_QK_WIDTH), pl.BlockSpec((1, RET_QK_WIDTH, tm), lambda b, i: (b, 0, i)),
                  row(RET_V_WIDTH), state, state, _const_spec(dm.shape), _const_spec(qs.shape),
                  row(RET_V_WIDTH), row(MLA_OUT_WIDTH),
                  row(RET_V_WIDTH), row(RET_V_WIDTH), _const_spec((1, RET_V_WIDTH)), _mod_spec(d, k_gate),
                  _const_spec(wr.shape), _const_spec(wm.shape), _const_spec(wo.shape)],
        out_specs=row(d),
        out_shape=jax.ShapeDtypeStruct(x.shape, F32),
        scratch_shapes=[pltpu.VMEM((tm, RET_V_WIDTH), BF16)],
        compiler_params=_cparams(("parallel", "parallel")),
        name="merge",
    )(x, rq, rkt, rv, sfs, sbs, dm, qs, rg, att, gr, gm, gn.reshape(1, -1), mod, wr, wm, wo)


def _rope_tables(length, n_freq):
    rows = length // GRID_W
    row = jnp.repeat(jnp.arange(rows, dtype=F32), GRID_W)
    col = jnp.tile(jnp.arange(GRID_W, dtype=F32), rows)
    inv_freq = jnp.power(ROPE_BASE, -jnp.arange(n_freq, dtype=F32) / n_freq)
    ang_r = row[:, None] * inv_freq[None, :]
    ang_c = col[:, None] * inv_freq[None, :]
    return jnp.cos(ang_r), jnp.sin(ang_r), jnp.cos(ang_c), jnp.sin(ang_c)


def _pair_tables(t):
    cos_r, sin_r, cos_c, sin_c = t
    return (jnp.concatenate([cos_r, cos_r, cos_c, cos_c], axis=-1),
            jnp.concatenate([-sin_r, sin_r, -sin_c, sin_c], axis=-1))


def _tables(length, rope):
    nr, nm = RET_QK_DIM // 4, MLA_ROPE_DIM // 4
    if rope:
        rt, mt = _rope_tables(length, nr), _rope_tables(length, nm)
    else:
        one, zero = jnp.ones((length, 1), F32), jnp.zeros((length, 1), F32)
        rt = tuple(jnp.tile(v, (1, nr)) for v in (one, zero, one, zero))
        mt = tuple(jnp.tile(v, (1, nm)) for v in (one, zero, one, zero))
    rc, rs = _pair_tables(rt)
    mc, ms = _pair_tables(mt)
    reps = LANES // RET_QK_DIM
    tail = LANES - MLA_NOPE_DIM - MLA_ROPE_DIM
    mcos = jnp.concatenate([jnp.ones((length, MLA_NOPE_DIM), F32), mc, jnp.ones((length, tail), F32)], axis=-1)
    msin = jnp.concatenate([jnp.zeros((length, MLA_NOPE_DIM), F32), ms, jnp.zeros((length, tail), F32)], axis=-1)
    mtab = jnp.concatenate(mt, axis=-1).T
    rtab = jnp.concatenate(rt, axis=-1).T
    return jnp.tile(rc, (1, reps)), jnp.tile(rs, (1, reps)), mcos, msin, mtab, rtab


def _prep_w_in(w_in):
    d = w_in.shape[0]
    o_kr = 2 * RET_QK_WIDTH + 2 * RET_V_WIDTH + MLA_Q_RANK + MLA_KV_RANK
    kr = jnp.concatenate([jnp.zeros((d, MLA_NOPE_DIM), F32), w_in[:, o_kr:o_kr + MLA_ROPE_DIM],
                          jnp.zeros((d, LANES - MLA_NOPE_DIM - MLA_ROPE_DIM), F32)], axis=1)
    return jnp.concatenate([w_in[:, :o_kr], kr, w_in[:, o_kr + MLA_ROPE_DIM:]], axis=1).astype(BF16)


def _prep_w_uq(w_uq):
    r = w_uq.shape[0]
    w = w_uq.reshape(r, MLA_HEADS, MLA_QK_DIM)
    w = jnp.pad(w, ((0, 0), (0, 0), (0, LANES - MLA_QK_DIM)))
    return w.reshape(r, MLA_HEADS * LANES).T.astype(BF16)


def _prep_w_ukv(w_ukv):
    r = w_ukv.shape[0]
    w = w_ukv.reshape(r, MLA_HEADS, MLA_NOPE_DIM + MLA_V_DIM)
    wk = jnp.pad(w[..., :MLA_NOPE_DIM], ((0, 0), (0, 0), (0, LANES - MLA_NOPE_DIM)))
    wv = w[..., MLA_NOPE_DIM:]
    return wk.reshape(r, MLA_HEADS * LANES).astype(BF16), wv.reshape(r, MLA_OUT_WIDTH).T.astype(BF16)


def kernel(x, c, ctx, c_ctx, w_ada, b_ada, ffn1_w1, ffn1_w3, ffn1_w2, ffn2_w1, ffn2_w3, ffn2_w2, w_in,
           ret_decay_fwd, ret_decay_bwd, ret_gn, mla_q_norm, mla_kv_norm, w_uq, w_ukv, w_ret_out,
           w_mla_out, w_o, final_norm):
    bsz, seq, d = x.shape
    lc = ctx.shape[1]
    depth = w_ada.shape[0]

    rows = -(-(bsz + 1) // 8) * 8
    cc = jnp.zeros((rows, d), F32).at[:bsz].set(c).at[bsz].set(c_ctx)
    mod_all = _mod_call(cc, w_ada, b_ada)

    tab_lat = _tables(seq, True)
    tab_ctx = _tables(lc, False)
    zero_state = jnp.zeros((bsz, RET_HEADS, RET_QK_DIM, RET_V_DIM), F32)

    xc = ctx
    for l in range(depth):
        last = l == depth - 1
        mod = mod_all[l, :bsz][:, None, :]
        mod_c = jnp.broadcast_to(mod_all[l, bsz][None, None, :], (bsz, 1, N_MOD * d))
        f1 = (ffn1_w1[l].astype(BF16), ffn1_w3[l].astype(BF16), ffn1_w2[l].astype(BF16))
        f2 = (ffn2_w1[l].astype(BF16), ffn2_w3[l].astype(BF16), ffn2_w2[l].astype(BF16))
        w_in_l = _prep_w_in(w_in[l])
        wuqt_l = _prep_w_uq(w_uq[l])
        wuk_l, wvt_l = _prep_w_ukv(w_ukv[l])
        wr, wm, wo = w_ret_out[l].astype(BF16), w_mla_out[l].astype(BF16), w_o[l].astype(BF16)

        x = _ffn_call(x, mod, 0, *f1)
        xc = _ffn_call(xc, mod_c, 0, *f1)

        wkt_l = w_in[l][:, RET_QK_WIDTH:2 * RET_QK_WIDTH].T.astype(BF16)
        mla_w = (w_in_l, mla_q_norm[l], mla_kv_norm[l], wuqt_l, wuk_l, wvt_l, wkt_l)
        rq, rkt, rv, rg, qt, km, vt, gr, gm = _inproj_call(x, mod, 3, *mla_w, tab_lat)
        crq, crkt, crv, crg, cqt, ckm, cvt, cgr, cgm = _inproj_call(xc, mod_c, 3, *mla_w, tab_ctx)

        dec = (ret_decay_fwd[l], ret_decay_bwd[l])
        csf, csb, sc_f, sc_b, dm, qs = _ret_state_call(crkt, crv, *dec, zero_state, zero_state)
        sfs, sbs, _, _, _, _ = _ret_state_call(rkt, rv, *dec, sc_f, sc_b)

        att = _attn_call(qt, km, vt, ckm, cvt)
        x = _merge_call(x, rq, rkt, rv, sfs, sbs, dm, qs, rg, att, gr, gm, ret_gn[l], mod, 5, wr, wm, wo)
        x = _ffn_call(x, mod, 6, *f2, final_gain=final_norm if last else None)
        if not last:
            att_c = _attn_call(cqt, ckm, cvt)
            xc = _merge_call(xc, crq, crkt, crv, csf, csb, dm, qs, crg, att_c, cgr, cgm, ret_gn[l],
                             mod_c, 5, wr, wm, wo)
            xc = _ffn_call(xc, mod_c, 6, *f2)
    return x
```

```python
import functools

import jax
import jax.numpy as jnp
from jax import lax
from jax.experimental import pallas as pl
from jax.experimental.pallas import tpu as pltpu

F32 = jnp.float32
BF16 = jnp.bfloat16

GRID_W = 64
EPS = 1e-6
ROPE_BASE = 10000.0
N_MOD = 9

RET_HEADS = 8
RET_QK_DIM = 64
RET_V_DIM = 128
RET_CHUNK = 128
RET_QK_WIDTH = RET_HEADS * RET_QK_DIM
RET_V_WIDTH = RET_HEADS * RET_V_DIM
RET_SCALE = RET_QK_DIM ** -0.5

MLA_HEADS = 8
MLA_Q_RANK = 384
MLA_KV_RANK = 256
MLA_NOPE_DIM = 64
MLA_ROPE_DIM = 32
MLA_V_DIM = 64
MLA_QK_DIM = MLA_NOPE_DIM + MLA_ROPE_DIM
MLA_OUT_WIDTH = MLA_HEADS * MLA_V_DIM
MLA_SCALE = MLA_QK_DIM ** -0.5

LANES = 128
MLA_HEAD_PAD = LANES
VMEM_LIMIT = 56 * 1024 * 1024
BF16_SUBLANES = 16
_VT_ROWS = MLA_V_DIM + BF16_SUBLANES
_Q_SCALE = MLA_SCALE * 1.4426950408889634
_NT = (((1,), (1,)), ((), ()))

_O_RQ = 0
_O_RK = _O_RQ + RET_QK_WIDTH
_O_RV = _O_RK + RET_QK_WIDTH
_O_RG = _O_RV + RET_V_WIDTH
_O_DQ = _O_RG + RET_V_WIDTH
_O_DKV = _O_DQ + MLA_Q_RANK
_O_KR = _O_DKV + MLA_KV_RANK
_O_GR = _O_KR + LANES
_O_GM = _O_GR + RET_V_WIDTH
_IN_W = _O_GM + RET_V_WIDTH


def _cparams(sem, vmem=VMEM_LIMIT):
    return pltpu.CompilerParams(dimension_semantics=sem, vmem_limit_bytes=vmem)


def _const_spec(shape):
    n = len(shape)
    return pl.BlockSpec(shape, lambda *_: (0,) * n, pipeline_mode=pl.Buffered(1))


def _rms(xf):
    return xf * lax.rsqrt(jnp.mean(xf * xf, axis=-1, keepdims=True) + EPS)


def _silu(a):
    return a * jax.nn.sigmoid(a)


def _mod_kernel(c_ref, w_ref, b_ref, o_ref):
    s = _silu(c_ref[...])
    o_ref[0] = jnp.dot(s, w_ref[0], preferred_element_type=F32,
                       precision=lax.Precision.HIGHEST) + b_ref[0]


def _mod_call(cc, w_ada, b_ada):
    depth, d, nd = w_ada.shape
    rows = cc.shape[0]
    tn = 1024
    return pl.pallas_call(
        _mod_kernel,
        grid=(depth, nd // tn),
        in_specs=[
            pl.BlockSpec((rows, d), lambda l, j: (0, 0)),
            pl.BlockSpec((1, d, tn), lambda l, j: (l, 0, j)),
            pl.BlockSpec((1, 1, tn), lambda l, j: (l, 0, j)),
        ],
        out_specs=pl.BlockSpec((1, rows, tn), lambda l, j: (l, 0, j)),
        out_shape=jax.ShapeDtypeStruct((depth, rows, nd), F32),
        compiler_params=_cparams(("arbitrary", "arbitrary")),
        name="mod",
    )(cc, w_ada, b_ada.reshape(depth, 1, nd))


def _ffn_kernel(x_ref, sh_ref, sc_ref, g_ref, w1_ref, w3_ref, w2_ref, *rest, final):
    if final:
        fn_ref, o_ref = rest
    else:
        (o_ref,) = rest
    xf = x_ref[0]
    h = _rms(xf) * (1.0 + sc_ref[0]) + sh_ref[0]
    hb = h.astype(BF16)
    a = jnp.dot(hb, w1_ref[...], preferred_element_type=F32)
    b = jnp.dot(hb, w3_ref[...], preferred_element_type=F32)
    u = (_silu(a) * b).astype(BF16)
    y = jnp.dot(u, w2_ref[...], preferred_element_type=F32)
    xn = xf + (0.5 * g_ref[0]) * y
    if final:
        xn = _rms(xn) * fn_ref[...]
    o_ref[0] = xn


def _mod_spec(d, k):
    return pl.BlockSpec((1, 1, d), lambda b, i: (b, 0, k))


def _ffn_call(x, mod, k0, w1, w3, w2, final_gain=None):
    bsz, seq, d = x.shape
    dff = w1.shape[1]
    tm = min(512, seq)
    final = final_gain is not None
    in_specs = [
        pl.BlockSpec((1, tm, d), lambda b, i: (b, i, 0)),
        _mod_spec(d, k0), _mod_spec(d, k0 + 1), _mod_spec(d, k0 + 2),
        _const_spec((d, dff)), _const_spec((d, dff)), _const_spec((dff, d)),
    ]
    args = [x, mod, mod, mod, w1, w3, w2]
    if final:
        in_specs.append(_const_spec((1, d)))
        args.append(final_gain.reshape(1, d))
    return pl.pallas_call(
        functools.partial(_ffn_kernel, final=final),
        grid=(bsz, seq // tm),
        in_specs=in_specs,
        out_specs=pl.BlockSpec((1, tm, d), lambda b, i: (b, i, 0)),
        out_shape=jax.ShapeDtypeStruct(x.shape, F32),
        compiler_params=_cparams(("parallel", "parallel")),
        name="ffn_final" if final else "ffn",
    )(*args)


def _rope(y, cos, sin, first_half, off):
    partner = jnp.where(first_half, pltpu.roll(y, LANES - off, 1), pltpu.roll(y, off, 1))
    return y * cos + partner * sin


def _inproj_kernel(x_ref, sh_ref, sc_ref, w_ref, qn_ref, kvn_ref, wuqt_ref, wuk_ref, wvt_ref, wkt_ref,
                   rcos_ref, rsin_ref, mcos_ref, msin_ref, mtab_ref, rtab_ref,
                   rq_ref, rk_ref, rv_ref, rg_ref, qt_ref, km_ref, vt_ref, gr_ref, gm_ref):
    tm = x_ref.shape[1]
    xf = x_ref[0]
    hb = (_rms(xf) * (1.0 + sc_ref[0]) + sh_ref[0]).astype(BF16)

    def proj(a, b):
        return jnp.dot(hb, w_ref[:, a:b], preferred_element_type=F32)

    lane = lax.broadcasted_iota(jnp.int32, (tm, LANES), 1)
    ret_first = (lane % (RET_QK_DIM // 2)) < (RET_QK_DIM // 4)
    mla_first = (lane % (MLA_ROPE_DIM // 2)) < (MLA_ROPE_DIM // 4)
    rcos, rsin = rcos_ref[...], rsin_ref[...]

    dq_raw = proj(_O_DQ, _O_DKV)
    dkv_raw = proj(_O_DKV, _O_KR)
    kr_raw = proj(_O_KR, _O_GR)
    q = proj(_O_RQ, _O_RK)
    kt = lax.dot_general(wkt_ref[...], hb, _NT, preferred_element_type=F32) * RET_SCALE
    rv_ref[0] = proj(_O_RV, _O_RG).astype(BF16)
    rg_ref[0] = proj(_O_RG, _O_DQ).astype(BF16)
    dq = (_rms(dq_raw) * qn_ref[...]).astype(BF16)
    dkv = (_rms(dkv_raw) * kvn_ref[...]).astype(BF16)
    qt = lax.dot_general(wuqt_ref[...], dq, _NT, preferred_element_type=F32) * _Q_SCALE
    kn = jnp.dot(dkv, wuk_ref[...], preferred_element_type=F32)
    vt = lax.dot_general(wvt_ref[...], dkv, _NT, preferred_element_type=F32)
    gr_ref[0] = proj(_O_GR, _O_GM).astype(BF16)
    gm_ref[0] = proj(_O_GM, _IN_W).astype(BF16)

    for g in range(RET_QK_WIDTH // LANES):
        sl = slice(g * LANES, (g + 1) * LANES)
        rq_ref[0, :, sl] = _rope(q[:, sl], rcos, rsin, ret_first, RET_QK_DIM // 4).astype(BF16)

    nr = RET_QK_DIM // 4
    rcos_r, rsin_r = rtab_ref[0:nr, :], rtab_ref[nr:2 * nr, :]
    rcos_c, rsin_c = rtab_ref[2 * nr:3 * nr, :], rtab_ref[3 * nr:4 * nr, :]
    for h in range(RET_HEADS):
        r0 = h * RET_QK_DIM
        a, b = kt[r0:r0 + nr], kt[r0 + nr:r0 + 2 * nr]
        c, d = kt[r0 + 2 * nr:r0 + 3 * nr], kt[r0 + 3 * nr:r0 + 4 * nr]
        head = jnp.concatenate([a * rcos_r - b * rsin_r, b * rcos_r + a * rsin_r,
                                c * rcos_c - d * rsin_c, d * rcos_c + c * rsin_c], axis=0)
        rk_ref[0, r0:r0 + RET_QK_DIM, :] = head.astype(BF16)

    nf = MLA_ROPE_DIM // 4
    cos_r, sin_r = mtab_ref[0:nf, :], mtab_ref[nf:2 * nf, :]
    cos_c, sin_c = mtab_ref[2 * nf:3 * nf, :], mtab_ref[3 * nf:4 * nf, :]
    for h in range(MLA_HEADS):
        r0 = h * LANES + MLA_NOPE_DIM
        a, b = qt[r0:r0 + nf], qt[r0 + nf:r0 + 2 * nf]
        c, d = qt[r0 + 2 * nf:r0 + 3 * nf], qt[r0 + 3 * nf:r0 + 4 * nf]
        head = jnp.concatenate([
            qt[h * LANES:r0],
            a * cos_r - b * sin_r, b * cos_r + a * sin_r,
            c * cos_c - d * sin_c, d * cos_c + c * sin_c,
            qt[r0 + MLA_ROPE_DIM:(h + 1) * LANES]], axis=0)
        qt_ref[0, h * LANES:(h + 1) * LANES, :] = head.astype(BF16)

    kr = _rope(kr_raw, mcos_ref[...], msin_ref[...], mla_first, MLA_ROPE_DIM // 4)
    for h in range(MLA_HEADS):
        sl = slice(h * LANES, (h + 1) * LANES)
        km_ref[0, :, sl] = (kn[:, sl] + kr).astype(BF16)
    pad = _VT_ROWS - MLA_V_DIM
    ones_row = (lax.broadcasted_iota(jnp.int32, (pad, tm), 0) == 0).astype(BF16)
    for h in range(MLA_HEADS):
        vt_ref[0, h * _VT_ROWS:h * _VT_ROWS + MLA_V_DIM, :] = vt[h * MLA_V_DIM:(h + 1) * MLA_V_DIM].astype(BF16)
        vt_ref[0, h * _VT_ROWS + MLA_V_DIM:(h + 1) * _VT_ROWS, :] = ones_row


def _inproj_call(x, mod, k0, w, q_norm, kv_norm, wuqt, wuk, wvt, wkt, tables):
    bsz, seq, d = x.shape
    tm = min(512, seq)
    row = lambda w_: pl.BlockSpec((1, tm, w_), lambda b, i: (b, i, 0))
    col = lambda r_: pl.BlockSpec((1, r_, tm), lambda b, i: (b, 0, i))
    tab = pl.BlockSpec((tm, LANES), lambda b, i: (i, 0))
    ttab = lambda r_: pl.BlockSpec((r_, tm), lambda b, i: (0, i))
    n_q, n_vt = MLA_HEADS * LANES, MLA_HEADS * _VT_ROWS
    shapes = [(seq, RET_QK_WIDTH), (RET_QK_WIDTH, seq), (seq, RET_V_WIDTH), (seq, RET_V_WIDTH),
              (n_q, seq), (seq, n_q), (n_vt, seq), (seq, RET_V_WIDTH), (seq, RET_V_WIDTH)]
    dtypes = (BF16,) * len(shapes)
    out_specs = [col(s[0]) if i in (1, 4, 6) else row(s[1]) for i, s in enumerate(shapes)]
    return pl.pallas_call(
        _inproj_kernel,
        grid=(bsz, seq // tm),
        in_specs=[
            row(d), _mod_spec(d, k0), _mod_spec(d, k0 + 1),
            _const_spec(w.shape), _const_spec((1, MLA_Q_RANK)), _const_spec((1, MLA_KV_RANK)),
            _const_spec(wuqt.shape), _const_spec(wuk.shape), _const_spec(wvt.shape), _const_spec(wkt.shape),
            tab, tab, tab, tab, ttab(MLA_ROPE_DIM), ttab(RET_QK_DIM),
        ],
        out_specs=out_specs,
        out_shape=[jax.ShapeDtypeStruct((bsz,) + s, dt) for s, dt in zip(shapes, dtypes)],
        compiler_params=_cparams(("parallel", "parallel")),
        name="inproj",
    )(x, mod, mod, w, q_norm.reshape(1, -1), kv_norm.reshape(1, -1), wuqt, wuk, wvt, wkt, *tables)


def _ret_state_kernel(kf_ref, vf_ref, kb_ref, vb_ref, decf_ref, decb_ref, sf0_ref, sb0_ref,
                      sfs_ref, sbs_ref, sf_ref, sb_ref, dm_ref, qs_ref,
                      st_f, st_b, ks_f, ks_b, cd_f, cd_b, *, group):
    b, s = pl.program_id(0), pl.program_id(1)
    c, dk, dv = RET_CHUNK, RET_QK_DIM, RET_V_DIM

    @pl.when(s == 0)
    def _():
        st_f[...] = sf0_ref[0]
        st_b[...] = sb0_ref[0]
        j = lax.broadcasted_iota(jnp.int32, (1, c), 1).astype(F32)
        for h in range(RET_HEADS):
            lgf = -jnp.exp(decf_ref[h])
            lgb = -jnp.exp(decb_ref[h])
            ks_f[h] = jnp.exp(lgf * (c - 1.0 - j))
            ks_b[h] = jnp.exp(lgb * j)
            cd_f[h] = jnp.broadcast_to(jnp.exp(lgf * c), (1, dv))
            cd_b[h] = jnp.broadcast_to(jnp.exp(lgb * c), (1, dv))

    @pl.when((b == 0) & (s == 0))
    def _():
        ii = lax.broadcasted_iota(jnp.int32, (c, c), 0).astype(F32)
        jj = lax.broadcasted_iota(jnp.int32, (c, c), 1).astype(F32)
        diff = ii - jj
        row = lax.broadcasted_iota(jnp.int32, (c, 2 * dk), 0).astype(F32)
        fwd_half = lax.broadcasted_iota(jnp.int32, (c, 2 * dk), 1) < dk
        for h in range(RET_HEADS):
            lgf = -jnp.exp(decf_ref[h])
            lgb = -jnp.exp(decb_ref[h])
            dm_ref[h] = (jnp.where(diff >= 0, jnp.exp(lgf * jnp.maximum(diff, 0.0)), 0.0)
                         + jnp.where(diff <= 0, jnp.exp(lgb * jnp.maximum(-diff, 0.0)), 0.0))
            qs_ref[h] = jnp.exp(jnp.where(fwd_half, lgf * (row + 1.0), lgb * (c - row)))

    kv = {}
    for g in range(group):
        gb = group - 1 - g
        for h in range(RET_HEADS):
            kf = (kf_ref[0, h * dk:(h + 1) * dk, g * c:(g + 1) * c].astype(F32) * ks_f[h]).astype(BF16)
            kv["f", g, h] = jnp.dot(kf, vf_ref[0, g * c:(g + 1) * c, h * dv:(h + 1) * dv],
                                    preferred_element_type=F32)
            kb = (kb_ref[0, h * dk:(h + 1) * dk, gb * c:(gb + 1) * c].astype(F32) * ks_b[h]).astype(BF16)
            kv["b", g, h] = jnp.dot(kb, vb_ref[0, gb * c:(gb + 1) * c, h * dv:(h + 1) * dv],
                                    preferred_element_type=F32)
    for h in range(RET_HEADS):
        sf, sb = st_f[h], st_b[h]
        for g in range(group):
            gb = group - 1 - g
            sfs_ref[0, g, h * dk:(h + 1) * dk, :] = sf.astype(BF16)
            sf = cd_f[h] * sf + kv["f", g, h]
            sbs_ref[0, gb, h * dk:(h + 1) * dk, :] = sb.astype(BF16)
            sb = cd_b[h] * sb + kv["b", g, h]
        st_f[h], st_b[h] = sf, sb

    @pl.when(s == pl.num_programs(1) - 1)
    def _():
        sf_ref[0] = st_f[...]
        sb_ref[0] = st_b[...]


def _ret_state_call(rkt, rv, dec_f, dec_b, s0_f, s0_b):
    bsz, seq, _ = rv.shape
    c = RET_CHUNK
    nc = seq // c
    group = min(4, nc)
    ns = nc // group
    h, dk, dv = RET_HEADS, RET_QK_DIM, RET_V_DIM
    st_spec = pl.BlockSpec((1, h, dk, dv), lambda b, s: (b, 0, 0, 0))
    dec_spec = pl.BlockSpec((h, 1, 1), lambda b, s: (0, 0, 0))
    tab = lambda w_: pl.BlockSpec((h, c, w_), lambda b, s: (0, 0, 0))
    return pl.pallas_call(
        functools.partial(_ret_state_kernel, group=group),
        grid=(bsz, ns),
        in_specs=[
            pl.BlockSpec((1, h * dk, group * c), lambda b, s: (b, 0, s)),
            pl.BlockSpec((1, group * c, h * dv), lambda b, s: (b, s, 0)),
            pl.BlockSpec((1, h * dk, group * c), lambda b, s: (b, 0, ns - 1 - s)),
            pl.BlockSpec((1, group * c, h * dv), lambda b, s: (b, ns - 1 - s, 0)),
            dec_spec, dec_spec, st_spec, st_spec],
        out_specs=[
            pl.BlockSpec((1, group, h * dk, dv), lambda b, s: (b, s, 0, 0)),
            pl.BlockSpec((1, group, h * dk, dv), lambda b, s: (b, ns - 1 - s, 0, 0)),
            st_spec, st_spec, tab(c), tab(2 * dk)],
        out_shape=[jax.ShapeDtypeStruct((bsz, nc, h * dk, dv), BF16)] * 2
        + [jax.ShapeDtypeStruct((bsz, h, dk, dv), F32)] * 2
        + [jax.ShapeDtypeStruct((h, c, c), F32), jax.ShapeDtypeStruct((h, c, 2 * dk), F32)],
        scratch_shapes=[
            pltpu.VMEM((h, dk, dv), F32), pltpu.VMEM((h, dk, dv), F32),
            pltpu.VMEM((h, 1, c), F32), pltpu.VMEM((h, 1, c), F32),
            pltpu.VMEM((h, 1, dv), F32), pltpu.VMEM((h, 1, dv), F32),
        ],
        compiler_params=_cparams(("arbitrary", "arbitrary")),
        name="ret_state",
    )(rkt, rv, rkt, rv, dec_f.reshape(h, 1, 1), dec_b.reshape(h, 1, 1), s0_f, s0_b)


def _attn_kernel(qt_ref, *refs, tq, tk, depth):
    o_ref = refs[-1]
    sources = [(refs[i], refs[i + 1]) for i in range(0, len(refs) - 1, 2)]
    chunks = [(k_ref, vt_ref, j) for k_ref, vt_ref in sources for j in range(k_ref.shape[1] // tk)]
    nk = len(chunks)
    heads = range(qt_ref.shape[1] // LANES)
    rounds = [(qi, n) for qi in range(qt_ref.shape[2] // tq) for n in range(nk)]

    def scores(hd, qi, n):
        k_ref, _, j = chunks[n]
        k = k_ref[0, j * tk:(j + 1) * tk, hd * LANES:(hd + 1) * LANES]
        qt = qt_ref[0, hd * LANES:(hd + 1) * LANES, qi * tq:(qi + 1) * tq]
        return jnp.dot(k, qt, preferred_element_type=F32)

    pending = {}
    for r in range(-depth, len(rounds)):
        if r + depth < len(rounds):
            for hd in heads:
                pending[hd, r + depth] = scores(hd, *rounds[r + depth])
        if r < 0:
            continue
        qi, n = rounds[r]
        if n == 0:
            m = [jnp.full((1, tq), -jnp.inf, F32) for _ in heads]
            acc = [jnp.zeros((_VT_ROWS, tq), F32) for _ in heads]
        _, vt_ref, j = chunks[n]
        pts, alphas = [], []
        for hd in heads:
            st = pending.pop((hd, r))
            m_new = jnp.maximum(m[hd], jnp.max(st, axis=0, keepdims=True))
            alphas.append(jnp.exp2(m[hd] - m_new))
            pts.append(jnp.exp2(st - m_new).astype(BF16))
            m[hd] = m_new
        for hd in heads:
            vt = vt_ref[0, hd * _VT_ROWS:(hd + 1) * _VT_ROWS, j * tk:(j + 1) * tk]
            acc[hd] = alphas[hd] * acc[hd] + jnp.dot(vt, pts[hd], preferred_element_type=F32)
        if n == nk - 1:
            outs = [a[:MLA_V_DIM] / a[MLA_V_DIM:MLA_V_DIM + 1] for a in acc]
            o_ref[0, qi * tq:(qi + 1) * tq, :] = jnp.concatenate(outs, axis=0).T.astype(o_ref.dtype)


def _attn_call(qt, *kv):
    bsz, _, seq = qt.shape
    tq = tk = 256
    tqs = min(2 * tq, seq)
    hps = 2
    in_specs = [pl.BlockSpec((1, hps * LANES, tqs), lambda b, h, i: (b, h, i))]
    for km, vt in zip(kv[::2], kv[1::2]):
        lk = km.shape[1]
        in_specs += [pl.BlockSpec((1, lk, hps * LANES), lambda b, h, i: (b, 0, h)),
                     pl.BlockSpec((1, hps * _VT_ROWS, lk), lambda b, h, i: (b, h, 0))]
    return pl.pallas_call(
        functools.partial(_attn_kernel, tq=tq, tk=tk, depth=2),
        grid=(bsz, MLA_HEADS // hps, seq // tqs),
        in_specs=in_specs,
        out_specs=pl.BlockSpec((1, tqs, hps * MLA_V_DIM), lambda b, h, i: (b, i, h)),
        out_shape=jax.ShapeDtypeStruct((bsz, seq, MLA_OUT_WIDTH), BF16),
        compiler_params=_cparams(("parallel", "parallel", "parallel")),
        name="attention",
    )(qt, *kv)


def _merge_kernel(x_ref, rq_ref, rkt_ref, rv_ref, sfs_ref, sbs_ref, dm_ref, qs_ref,
                  rg_ref, at_ref, gr_ref, gm_ref, gn_ref, g2_ref,
                  wr_ref, wm_ref, wo_ref, o_ref, a_scr):
    c, dk, dv = RET_CHUNK, RET_QK_DIM, RET_V_DIM
    units = [(ci, h) for ci in range(x_ref.shape[1] // c) for h in range(RET_HEADS)]
    lane = lax.broadcasted_iota(jnp.int32, (c, 2 * dk), 1)
    low = lane < dk

    def q_pair(ci, h):
        g = h // 2
        return rq_ref[0, ci * c:(ci + 1) * c, g * 2 * dk:(g + 1) * 2 * dk]

    scores = {}
    for ci, h in units:
        mine = low if h % 2 == 0 else jnp.logical_not(low)
        qm = jnp.where(mine, q_pair(ci, h), jnp.zeros((), BF16))
        g = h // 2
        kt = rkt_ref[0, g * 2 * dk:(g + 1) * 2 * dk, ci * c:(ci + 1) * c]
        scores[ci, h] = jnp.dot(qm, kt, preferred_element_type=F32)
    outs = {}
    for ci, h in units:
        p = (scores.pop((ci, h)) * dm_ref[h]).astype(BF16)
        qf = q_pair(ci, h).astype(F32)
        qr = pltpu.roll(qf, dk, 1)
        both = jnp.where(low, qf, qr) if h % 2 == 0 else jnp.where(low, qr, qf)
        lhs = jnp.concatenate([p, (both * qs_ref[h]).astype(BF16)], axis=1)
        rhs = jnp.concatenate([rv_ref[0, ci * c:(ci + 1) * c, h * dv:(h + 1) * dv],
                               sfs_ref[0, ci, h * dk:(h + 1) * dk, :],
                               sbs_ref[0, ci, h * dk:(h + 1) * dk, :]], axis=0)
        outs[ci, h] = jnp.dot(lhs, rhs, preferred_element_type=F32)
    mla_l = jnp.dot(at_ref[0], wm_ref[...], preferred_element_type=F32)
    for ci, h in units:
        rows, sl = slice(ci * c, (ci + 1) * c), slice(h * dv, (h + 1) * dv)
        o = outs.pop((ci, h))
        d = o - jnp.mean(o, axis=-1, keepdims=True)
        nrm = d * lax.rsqrt(jnp.mean(d * d, axis=-1, keepdims=True) + EPS) * gn_ref[:, sl]
        a_scr[rows, sl] = (_silu(rg_ref[0, rows, sl].astype(F32)) * nrm).astype(BF16)
    ret_l = jnp.dot(a_scr[...], wr_ref[...], preferred_element_type=F32)
    mrg = (jax.nn.sigmoid(gr_ref[0].astype(F32)) * ret_l
           + jax.nn.sigmoid(gm_ref[0].astype(F32)) * mla_l)
    out = jnp.dot(mrg.astype(BF16), wo_ref[...], preferred_element_type=F32)
    o_ref[0] = x_ref[0] + g2_ref[0] * out


def _merge_call(x, rq, rkt, rv, sfs, sbs, dm, qs, rg, att, gr, gm, gn, mod, k_gate, wr, wm, wo):
    bsz, seq, d = x.shape
    tm = min(512, seq)
    nch = tm // RET_CHUNK
    row = lambda w_: pl.BlockSpec((1, tm, w_), lambda b, i: (b, i, 0))
    state = pl.BlockSpec((1, nch, RET_QK_WIDTH, RET_V_DIM), lambda b, i: (b, i, 0, 0))
    return pl.pallas_call(
        _merge_kernel,
        grid=(bsz, seq // tm),
        in_specs=[row(d), row(RET_QK_WIDTH), pl.BlockSpec((1, RET_QK_WIDTH, tm), lambda b, i: (b, 0, i)),
                  row(RET_V_WIDTH), state, state, _const_spec(dm.shape), _const_spec(qs.shape),
                  row(RET_V_WIDTH), row(MLA_OUT_WIDTH),
                  row(RET_V_WIDTH), row(RET_V_WIDTH), _const_spec((1, RET_V_WIDTH)), _mod_spec(d, k_gate),
                  _const_spec(wr.shape), _const_spec(wm.shape), _const_spec(wo.shape)],
        out_specs=row(d),
        out_shape=jax.ShapeDtypeStruct(x.shape, F32),
        scratch_shapes=[pltpu.VMEM((tm, RET_V_WIDTH), BF16)],
        compiler_params=_cparams(("parallel", "parallel")),
        name="merge",
    )(x, rq, rkt, rv, sfs, sbs, dm, qs, rg, att, gr, gm, gn.reshape(1, -1), mod, wr, wm, wo)


def _rope_tables(length, n_freq):
    rows = length // GRID_W
    row = jnp.repeat(jnp.arange(rows, dtype=F32), GRID_W)
    col = jnp.tile(jnp.arange(GRID_W, dtype=F32), rows)
    inv_freq = jnp.power(ROPE_BASE, -jnp.arange(n_freq, dtype=F32) / n_freq)
    ang_r = row[:, None] * inv_freq[None, :]
    ang_c = col[:, None] * inv_freq[None, :]
    return jnp.cos(ang_r), jnp.sin(ang_r), jnp.cos(ang_c), jnp.sin(ang_c)


def _pair_tables(t):
    cos_r, sin_r, cos_c, sin_c = t
    return (jnp.concatenate([cos_r, cos_r, cos_c, cos_c], axis=-1),
            jnp.concatenate([-sin_r, sin_r, -sin_c, sin_c], axis=-1))


def _tables(length, rope):
    nr, nm = RET_QK_DIM // 4, MLA_ROPE_DIM // 4
    if rope:
        rt, mt = _rope_tables(length, nr), _rope_tables(length, nm)
    else:
        one, zero = jnp.ones((length, 1), F32), jnp.zeros((length, 1), F32)
        rt = tuple(jnp.tile(v, (1, nr)) for v in (one, zero, one, zero))
        mt = tuple(jnp.tile(v, (1, nm)) for v in (one, zero, one, zero))
    rc, rs = _pair_tables(rt)
    mc, ms = _pair_tables(mt)
    reps = LANES // RET_QK_DIM
    tail = LANES - MLA_NOPE_DIM - MLA_ROPE_DIM
    mcos = jnp.concatenate([jnp.ones((length, MLA_NOPE_DIM), F32), mc, jnp.ones((length, tail), F32)], axis=-1)
    msin = jnp.concatenate([jnp.zeros((length, MLA_NOPE_DIM), F32), ms, jnp.zeros((length, tail), F32)], axis=-1)
    mtab = jnp.concatenate(mt, axis=-1).T
    rtab = jnp.concatenate(rt, axis=-1).T
    return jnp.tile(rc, (1, reps)), jnp.tile(rs, (1, reps)), mcos, msin, mtab, rtab


def _prep_w_in(w_in):
    d = w_in.shape[0]
    o_kr = 2 * RET_QK_WIDTH + 2 * RET_V_WIDTH + MLA_Q_RANK + MLA_KV_RANK
    kr = jnp.concatenate([jnp.zeros((d, MLA_NOPE_DIM), F32), w_in[:, o_kr:o_kr + MLA_ROPE_DIM],
                          jnp.zeros((d, LANES - MLA_NOPE_DIM - MLA_ROPE_DIM), F32)], axis=1)
    return jnp.concatenate([w_in[:, :o_kr], kr, w_in[:, o_kr + MLA_ROPE_DIM:]], axis=1).astype(BF16)


def _prep_w_uq(w_uq):
    r = w_uq.shape[0]
    w = w_uq.reshape(r, MLA_HEADS, MLA_QK_DIM)
    w = jnp.pad(w, ((0, 0), (0, 0), (0, LANES - MLA_QK_DIM)))
    return w.reshape(r, MLA_HEADS * LANES).T.astype(BF16)


def _prep_w_ukv(w_ukv):
    r = w_ukv.shape[0]
    w = w_ukv.reshape(r, MLA_HEADS, MLA_NOPE_DIM + MLA_V_DIM)
    wk = jnp.pad(w[..., :MLA_NOPE_DIM], ((0, 0), (0, 0), (0, LANES - MLA_NOPE_DIM)))
    wv = w[..., MLA_NOPE_DIM:]
    return wk.reshape(r, MLA_HEADS * LANES).astype(BF16), wv.reshape(r, MLA_OUT_WIDTH).T.astype(BF16)


def kernel(x, c, ctx, c_ctx, w_ada, b_ada, ffn1_w1, ffn1_w3, ffn1_w2, ffn2_w1, ffn2_w3, ffn2_w2, w_in,
           ret_decay_fwd, ret_decay_bwd, ret_gn, mla_q_norm, mla_kv_norm, w_uq, w_ukv, w_ret_out,
           w_mla_out, w_o, final_norm):
    bsz, seq, d = x.shape
    lc = ctx.shape[1]
    depth = w_ada.shape[0]

    rows = -(-(bsz + 1) // 8) * 8
    cc = jnp.zeros((rows, d), F32).at[:bsz].set(c).at[bsz].set(c_ctx)
    mod_all = _mod_call(cc, w_ada, b_ada)

    tab_lat = _tables(seq, True)
    tab_ctx = _tables(lc, False)
    zero_state = jnp.zeros((bsz, RET_HEADS, RET_QK_DIM, RET_V_DIM), F32)

    xc = ctx
    for l in range(depth):
        last = l == depth - 1
        mod = mod_all[l, :bsz][:, None, :]
        mod_c = jnp.broadcast_to(mod_all[l, bsz][None, None, :], (bsz, 1, N_MOD * d))
        f1 = (ffn1_w1[l].astype(BF16), ffn1_w3[l].astype(BF16), ffn1_w2[l].astype(BF16))
        f2 = (ffn2_w1[l].astype(BF16), ffn2_w3[l].astype(BF16), ffn2_w2[l].astype(BF16))
        w_in_l = _prep_w_in(w_in[l])
        wuqt_l = _prep_w_uq(w_uq[l])
        wuk_l, wvt_l = _prep_w_ukv(w_ukv[l])
        wr, wm, wo = w_ret_out[l].astype(BF16), w_mla_out[l].astype(BF16), w_o[l].astype(BF16)

        x = _ffn_call(x, mod, 0, *f1)
        xc = _ffn_call(xc, mod_c, 0, *f1)

        wkt_l = w_in[l][:, RET_QK_WIDTH:2 * RET_QK_WIDTH].T.astype(BF16)
        mla_w = (w_in_l, mla_q_norm[l], mla_kv_norm[l], wuqt_l, wuk_l, wvt_l, wkt_l)
        rq, rkt, rv, rg, qt, km, vt, gr, gm = _inproj_call(x, mod, 3, *mla_w, tab_lat)
        crq, crkt, crv, crg, cqt, ckm, cvt, cgr, cgm = _inproj_call(xc, mod_c, 3, *mla_w, tab_ctx)

        dec = (ret_decay_fwd[l], ret_decay_bwd[l])
        csf, csb, sc_f, sc_b, dm, qs = _ret_state_call(crkt, crv, *dec, zero_state, zero_state)
        sfs, sbs, _, _, _, _ = _ret_state_call(rkt, rv, *dec, sc_f, sc_b)

        att = _attn_call(qt, km, vt, ckm, cvt)
        x = _merge_call(x, rq, rkt, rv, sfs, sbs, dm, qs, rg, att, gr, gm, ret_gn[l], mod, 5, wr, wm, wo)
        x = _ffn_call(x, mod, 6, *f2, final_gain=final_norm if last else None)
        if not last:
            att_c = _attn_call(cqt, ckm, cvt)
            xc = _merge_call(xc, crq, crkt, crv, csf, csb, dm, qs, crg, att_c, cgr, cgm, ret_gn[l],
                             mod_c, 5, wr, wm, wo)
            xc = _ffn_call(xc, mod_c, 6, *f2)
    return x
```

```python
import functools

import jax
import jax.numpy as jnp
from jax import lax
from jax.experimental import pallas as pl
from jax.experimental.pallas import tpu as pltpu

F32 = jnp.float32
BF16 = jnp.bfloat16

GRID_W = 64
EPS = 1e-6
ROPE_BASE = 10000.0
N_MOD = 9

RET_HEADS = 8
RET_QK_DIM = 64
RET_V_DIM = 128
RET_CHUNK = 128
RET_QK_WIDTH = RET_HEADS * RET_QK_DIM
RET_V_WIDTH = RET_HEADS * RET_V_DIM
RET_SCALE = RET_QK_DIM ** -0.5

MLA_HEADS = 8
MLA_Q_RANK = 384
MLA_KV_RANK = 256
MLA_NOPE_DIM = 64
MLA_ROPE_DIM = 32
MLA_V_DIM = 64
MLA_QK_DIM = MLA_NOPE_DIM + MLA_ROPE_DIM
MLA_OUT_WIDTH = MLA_HEADS * MLA_V_DIM
MLA_SCALE = MLA_QK_DIM ** -0.5

LANES = 128
MLA_HEAD_PAD = LANES
VMEM_LIMIT = 56 * 1024 * 1024
BF16_SUBLANES = 16
_VT_ROWS = MLA_V_DIM + BF16_SUBLANES
_Q_SCALE = MLA_SCALE * 1.4426950408889634
_NT = (((1,), (1,)), ((), ()))

_O_RQ = 0
_O_RK = _O_RQ + RET_QK_WIDTH
_O_RV = _O_RK + RET_QK_WIDTH
_O_RG = _O_RV + RET_V_WIDTH
_O_DQ = _O_RG + RET_V_WIDTH
_O_DKV = _O_DQ + MLA_Q_RANK
_O_KR = _O_DKV + MLA_KV_RANK


def _cparams(sem, vmem=VMEM_LIMIT):
    return pltpu.CompilerParams(dimension_semantics=sem, vmem_limit_bytes=vmem)


def _const_spec(shape):
    n = len(shape)
    return pl.BlockSpec(shape, lambda *_: (0,) * n, pipeline_mode=pl.Buffered(1))


def _rms(xf):
    return xf * lax.rsqrt(jnp.mean(xf * xf, axis=-1, keepdims=True) + EPS)


def _sigmoid(a):
    return 0.5 * jnp.tanh(0.5 * a) + 0.5


def _silu(a):
    return a * _sigmoid(a)


def _mod_kernel(c_ref, w_ref, b_ref, o_ref):
    s = _silu(c_ref[...])
    o_ref[0] = jnp.dot(s, w_ref[0], preferred_element_type=F32,
                       precision=lax.Precision.HIGHEST) + b_ref[0]


def _mod_call(cc, w_ada, b_ada):
    depth, d, nd = w_ada.shape
    rows = cc.shape[0]
    tn = 1024
    return pl.pallas_call(
        _mod_kernel,
        grid=(depth, nd // tn),
        in_specs=[
            pl.BlockSpec((rows, d), lambda l, j: (0, 0)),
            pl.BlockSpec((1, d, tn), lambda l, j: (l, 0, j)),
            pl.BlockSpec((1, 1, tn), lambda l, j: (l, 0, j)),
        ],
        out_specs=pl.BlockSpec((1, rows, tn), lambda l, j: (l, 0, j)),
        out_shape=jax.ShapeDtypeStruct((depth, rows, nd), F32),
        compiler_params=_cparams(("arbitrary", "arbitrary")),
        name="mod",
    )(cc, w_ada, b_ada.reshape(depth, 1, nd))


def _ffn_kernel(x_ref, sh_ref, sc_ref, g_ref, w1_ref, w3_ref, w2_ref, *rest, final):
    if final:
        fn_ref, o_ref = rest
    else:
        (o_ref,) = rest
    xf = x_ref[0]
    h = _rms(xf) * (1.0 + sc_ref[0]) + sh_ref[0]
    hb = h.astype(BF16)
    a = jnp.dot(hb, w1_ref[...], preferred_element_type=F32)
    b = jnp.dot(hb, w3_ref[...], preferred_element_type=F32)
    u = (_silu(a) * b).astype(BF16)
    y = jnp.dot(u, w2_ref[...], preferred_element_type=F32)
    xn = xf + (0.5 * g_ref[0]) * y
    if final:
        xn = _rms(xn) * fn_ref[...]
    o_ref[0] = xn


def _mod_spec(d, k):
    return pl.BlockSpec((1, 1, d), lambda b, i: (b, 0, k))


def _ffn_call(x, mod, k0, w1, w3, w2, final_gain=None):
    bsz, seq, d = x.shape
    dff = w1.shape[1]
    tm = min(512, seq)
    final = final_gain is not None
    in_specs = [
        pl.BlockSpec((1, tm, d), lambda b, i: (b, i, 0)),
        _mod_spec(d, k0), _mod_spec(d, k0 + 1), _mod_spec(d, k0 + 2),
        _const_spec((d, dff)), _const_spec((d, dff)), _const_spec((dff, d)),
    ]
    args = [x, mod, mod, mod, w1, w3, w2]
    if final:
        in_specs.append(_const_spec((1, d)))
        args.append(final_gain.reshape(1, d))
    return pl.pallas_call(
        functools.partial(_ffn_kernel, final=final),
        grid=(bsz, seq // tm),
        in_specs=in_specs,
        out_specs=pl.BlockSpec((1, tm, d), lambda b, i: (b, i, 0)),
        out_shape=jax.ShapeDtypeStruct(x.shape, F32),
        compiler_params=_cparams(("parallel", "parallel")),
        name="ffn_final" if final else "ffn",
    )(*args)


def _rope(y, cos, sin, first_half, off):
    partner = jnp.where(first_half, pltpu.roll(y, LANES - off, 1), pltpu.roll(y, off, 1))
    return y * cos + partner * sin


def _inproj_kernel(x_ref, sh_ref, sc_ref, w_ref, wkr_ref, wg_ref, qn_ref, kvn_ref,
                   wuqt_ref, wuk_ref, wvt_ref, wkt_ref,
                   rcos_ref, rsin_ref, mcos_ref, msin_ref, mtab_ref, rtab_ref,
                   rq_ref, rk_ref, rv_ref, rg_ref, qt_ref, km_ref, vt_ref, gr_ref, gm_ref):
    tm = x_ref.shape[1]
    xf = x_ref[0]
    hb = (_rms(xf) * (1.0 + sc_ref[0]) + sh_ref[0]).astype(BF16)

    def proj(a, b):
        return jnp.dot(hb, w_ref[:, a:b], preferred_element_type=F32)

    lane = lax.broadcasted_iota(jnp.int32, (tm, LANES), 1)
    ret_first = (lane % (RET_QK_DIM // 2)) < (RET_QK_DIM // 4)
    mla_first = (lane % (MLA_ROPE_DIM // 2)) < (MLA_ROPE_DIM // 4)
    rcos, rsin = rcos_ref[...], rsin_ref[...]

    dq_raw = proj(_O_DQ, _O_DKV)
    dkv_raw = proj(_O_DKV, _O_KR)
    kr_raw = jnp.dot(hb, wkr_ref[...], preferred_element_type=F32)
    q = proj(_O_RQ, _O_RK)
    kt = lax.dot_general(wkt_ref[...], hb, _NT, preferred_element_type=F32) * RET_SCALE
    rv_ref[0] = proj(_O_RV, _O_RG).astype(BF16)
    rg_ref[0] = proj(_O_RG, _O_DQ).astype(BF16)
    dq = (_rms(dq_raw) * qn_ref[...]).astype(BF16)
    dkv = (_rms(dkv_raw) * kvn_ref[...]).astype(BF16)
    qt = lax.dot_general(wuqt_ref[...], dq, _NT, preferred_element_type=F32) * _Q_SCALE
    kn = jnp.dot(dkv, wuk_ref[...], preferred_element_type=F32)
    vt = lax.dot_general(wvt_ref[...], dkv, _NT, preferred_element_type=F32)
    gr_ref[0] = jnp.dot(hb, wg_ref[:, :RET_V_WIDTH], preferred_element_type=F32).astype(BF16)
    gm_ref[0] = jnp.dot(hb, wg_ref[:, RET_V_WIDTH:], preferred_element_type=F32).astype(BF16)

    for g in range(RET_QK_WIDTH // LANES):
        sl = slice(g * LANES, (g + 1) * LANES)
        rq_ref[0, :, sl] = _rope(q[:, sl], rcos, rsin, ret_first, RET_QK_DIM // 4).astype(BF16)

    nr = RET_QK_DIM // 4
    rcos_r, rsin_r = rtab_ref[0:nr, :], rtab_ref[nr:2 * nr, :]
    rcos_c, rsin_c = rtab_ref[2 * nr:3 * nr, :], rtab_ref[3 * nr:4 * nr, :]
    for h in range(RET_HEADS):
        r0 = h * RET_QK_DIM
        a, b = kt[r0:r0 + nr], kt[r0 + nr:r0 + 2 * nr]
        c, d = kt[r0 + 2 * nr:r0 + 3 * nr], kt[r0 + 3 * nr:r0 + 4 * nr]
        head = jnp.concatenate([a * rcos_r - b * rsin_r, b * rcos_r + a * rsin_r,
                                c * rcos_c - d * rsin_c, d * rcos_c + c * rsin_c], axis=0)
        rk_ref[0, r0:r0 + RET_QK_DIM, :] = head.astype(BF16)

    nf = MLA_ROPE_DIM // 4
    cos_r, sin_r = mtab_ref[0:nf, :], mtab_ref[nf:2 * nf, :]
    cos_c, sin_c = mtab_ref[2 * nf:3 * nf, :], mtab_ref[3 * nf:4 * nf, :]
    for h in range(MLA_HEADS):
        r0 = h * LANES + MLA_NOPE_DIM
        a, b = qt[r0:r0 + nf], qt[r0 + nf:r0 + 2 * nf]
        c, d = qt[r0 + 2 * nf:r0 + 3 * nf], qt[r0 + 3 * nf:r0 + 4 * nf]
        head = jnp.concatenate([
            qt[h * LANES:r0],
            a * cos_r - b * sin_r, b * cos_r + a * sin_r,
            c * cos_c - d * sin_c, d * cos_c + c * sin_c,
            qt[r0 + MLA_ROPE_DIM:(h + 1) * LANES]], axis=0)
        qt_ref[0, h * LANES:(h + 1) * LANES, :] = head.astype(BF16)

    kr = _rope(kr_raw, mcos_ref[...], msin_ref[...], mla_first, MLA_ROPE_DIM // 4)
    for h in range(MLA_HEADS):
        sl = slice(h * LANES, (h + 1) * LANES)
        km_ref[0, :, sl] = (kn[:, sl] + kr).astype(BF16)
    pad = _VT_ROWS - MLA_V_DIM
    ones_row = (lax.broadcasted_iota(jnp.int32, (pad, tm), 0) == 0).astype(BF16)
    for h in range(MLA_HEADS):
        vt_ref[0, h * _VT_ROWS:h * _VT_ROWS + MLA_V_DIM, :] = vt[h * MLA_V_DIM:(h + 1) * MLA_V_DIM].astype(BF16)
        vt_ref[0, h * _VT_ROWS + MLA_V_DIM:(h + 1) * _VT_ROWS, :] = ones_row


def _inproj_call(x, mod, k0, w, wkr, wg, q_norm, kv_norm, wuqt, wuk, wvt, wkt, tables):
    bsz, seq, d = x.shape
    tm = min(512, seq)
    row = lambda w_: pl.BlockSpec((1, tm, w_), lambda b, i: (b, i, 0))
    col = lambda r_: pl.BlockSpec((1, r_, tm), lambda b, i: (b, 0, i))
    tab = pl.BlockSpec((tm, LANES), lambda b, i: (i, 0))
    ttab = lambda r_: pl.BlockSpec((r_, tm), lambda b, i: (0, i))
    n_q, n_vt = MLA_HEADS * LANES, MLA_HEADS * _VT_ROWS
    shapes = [(seq, RET_QK_WIDTH), (RET_QK_WIDTH, seq), (seq, RET_V_WIDTH), (seq, RET_V_WIDTH),
              (n_q, seq), (seq, n_q), (n_vt, seq), (seq, RET_V_WIDTH), (seq, RET_V_WIDTH)]
    dtypes = (BF16,) * len(shapes)
    out_specs = [col(s[0]) if i in (1, 4, 6) else row(s[1]) for i, s in enumerate(shapes)]
    return pl.pallas_call(
        _inproj_kernel,
        grid=(bsz, seq // tm),
        in_specs=[
            row(d), _mod_spec(d, k0), _mod_spec(d, k0 + 1),
            _const_spec(w.shape), _const_spec(wkr.shape), _const_spec(wg.shape),
            _const_spec((1, MLA_Q_RANK)), _const_spec((1, MLA_KV_RANK)),
            _const_spec(wuqt.shape), _const_spec(wuk.shape), _const_spec(wvt.shape), _const_spec(wkt.shape),
            tab, tab, tab, tab, ttab(MLA_ROPE_DIM), ttab(RET_QK_DIM),
        ],
        out_specs=out_specs,
        out_shape=[jax.ShapeDtypeStruct((bsz,) + s, dt) for s, dt in zip(shapes, dtypes)],
        compiler_params=_cparams(("parallel", "parallel")),
        name="inproj",
    )(x, mod, mod, w, wkr, wg, q_norm.reshape(1, -1), kv_norm.reshape(1, -1), wuqt, wuk, wvt, wkt, *tables)


def _ret_state_kernel(kf_ref, vf_ref, kb_ref, vb_ref, decf_ref, decb_ref, sf0_ref, sb0_ref,
                      sfs_ref, sbs_ref, sf_ref, sb_ref, dm_ref, qs_ref,
                      st_f, st_b, ks_f, ks_b, cd_f, cd_b, *, group):
    b, s = pl.program_id(0), pl.program_id(1)
    c, dk, dv = RET_CHUNK, RET_QK_DIM, RET_V_DIM

    @pl.when(s == 0)
    def _():
        st_f[...] = sf0_ref[0]
        st_b[...] = sb0_ref[0]
        j = lax.broadcasted_iota(jnp.int32, (1, c), 1).astype(F32)
        for h in range(RET_HEADS):
            lgf = -jnp.exp(decf_ref[h])
            lgb = -jnp.exp(decb_ref[h])
            ks_f[h] = jnp.exp(lgf * (c - 1.0 - j))
            ks_b[h] = jnp.exp(lgb * j)
            cd_f[h] = jnp.broadcast_to(jnp.exp(lgf * c), (1, dv))
            cd_b[h] = jnp.broadcast_to(jnp.exp(lgb * c), (1, dv))

    @pl.when((b == 0) & (s == 0))
    def _():
        ii = lax.broadcasted_iota(jnp.int32, (c, c), 0).astype(F32)
        jj = lax.broadcasted_iota(jnp.int32, (c, c), 1).astype(F32)
        diff = ii - jj
        row = lax.broadcasted_iota(jnp.int32, (c, 2 * dk), 0).astype(F32)
        fwd_half = lax.broadcasted_iota(jnp.int32, (c, 2 * dk), 1) < dk
        for h in range(RET_HEADS):
            lgf = -jnp.exp(decf_ref[h])
            lgb = -jnp.exp(decb_ref[h])
            dm_ref[h] = (jnp.where(diff >= 0, jnp.exp(lgf * jnp.maximum(diff, 0.0)), 0.0)
                         + jnp.where(diff <= 0, jnp.exp(lgb * jnp.maximum(-diff, 0.0)), 0.0))
            qs_ref[h] = jnp.exp(jnp.where(fwd_half, lgf * (row + 1.0), lgb * (c - row)))

    kv = {}
    for g in range(group):
        gb = group - 1 - g
        for h in range(RET_HEADS):
            kf = (kf_ref[0, h * dk:(h + 1) * dk, g * c:(g + 1) * c].astype(F32) * ks_f[h]).astype(BF16)
            kv["f", g, h] = jnp.dot(kf, vf_ref[0, g * c:(g + 1) * c, h * dv:(h + 1) * dv],
                                    preferred_element_type=F32)
            kb = (kb_ref[0, h * dk:(h + 1) * dk, gb * c:(gb + 1) * c].astype(F32) * ks_b[h]).astype(BF16)
            kv["b", g, h] = jnp.dot(kb, vb_ref[0, gb * c:(gb + 1) * c, h * dv:(h + 1) * dv],
                                    preferred_element_type=F32)
    for h in range(RET_HEADS):
        sf, sb = st_f[h], st_b[h]
        for g in range(group):
            gb = group - 1 - g
            sfs_ref[0, g, h * dk:(h + 1) * dk, :] = sf.astype(BF16)
            sf = cd_f[h] * sf + kv["f", g, h]
            sbs_ref[0, gb, h * dk:(h + 1) * dk, :] = sb.astype(BF16)
            sb = cd_b[h] * sb + kv["b", g, h]
        st_f[h], st_b[h] = sf, sb

    @pl.when(s == pl.num_programs(1) - 1)
    def _():
        sf_ref[0] = st_f[...]
        sb_ref[0] = st_b[...]


def _ret_state_call(rkt, rv, dec_f, dec_b, s0_f, s0_b):
    bsz, seq, _ = rv.shape
    c = RET_CHUNK
    nc = seq // c
    group = min(8, nc)
    ns = nc // group
    h, dk, dv = RET_HEADS, RET_QK_DIM, RET_V_DIM
    st_spec = pl.BlockSpec((1, h, dk, dv), lambda b, s: (b, 0, 0, 0))
    dec_spec = pl.BlockSpec((h, 1, 1), lambda b, s: (0, 0, 0))
    tab = lambda w_: pl.BlockSpec((h, c, w_), lambda b, s: (0, 0, 0))
    return pl.pallas_call(
        functools.partial(_ret_state_kernel, group=group),
        grid=(bsz, ns),
        in_specs=[
            pl.BlockSpec((1, h * dk, group * c), lambda b, s: (b, 0, s)),
            pl.BlockSpec((1, group * c, h * dv), lambda b, s: (b, s, 0)),
            pl.BlockSpec((1, h * dk, group * c), lambda b, s: (b, 0, ns - 1 - s)),
            pl.BlockSpec((1, group * c, h * dv), lambda b, s: (b, ns - 1 - s, 0)),
            dec_spec, dec_spec, st_spec, st_spec],
        out_specs=[
            pl.BlockSpec((1, group, h * dk, dv), lambda b, s: (b, s, 0, 0)),
            pl.BlockSpec((1, group, h * dk, dv), lambda b, s: (b, ns - 1 - s, 0, 0)),
            st_spec, st_spec, tab(c), tab(2 * dk)],
        out_shape=[jax.ShapeDtypeStruct((bsz, nc, h * dk, dv), BF16)] * 2
        + [jax.ShapeDtypeStruct((bsz, h, dk, dv), F32)] * 2
        + [jax.ShapeDtypeStruct((h, c, c), F32), jax.ShapeDtypeStruct((h, c, 2 * dk), F32)],
        scratch_shapes=[
            pltpu.VMEM((h, dk, dv), F32), pltpu.VMEM((h, dk, dv), F32),
            pltpu.VMEM((h, 1, c), F32), pltpu.VMEM((h, 1, c), F32),
            pltpu.VMEM((h, 1, dv), F32), pltpu.VMEM((h, 1, dv), F32),
        ],
        compiler_params=_cparams(("arbitrary", "arbitrary")),
        name="ret_state",
    )(rkt, rv, rkt, rv, dec_f.reshape(h, 1, 1), dec_b.reshape(h, 1, 1), s0_f, s0_b)


def _attn_kernel(qt_ref, *refs, tq, tk, depth):
    o_ref = refs[-1]
    sources = [(refs[i], refs[i + 1]) for i in range(0, len(refs) - 1, 2)]
    chunks = [(k_ref, vt_ref, j) for k_ref, vt_ref in sources for j in range(k_ref.shape[1] // tk)]
    nk = len(chunks)
    heads = range(qt_ref.shape[1] // LANES)
    rounds = [(qi, n) for qi in range(qt_ref.shape[2] // tq) for n in range(nk)]

    def scores(hd, qi, n):
        k_ref, _, j = chunks[n]
        k = k_ref[0, j * tk:(j + 1) * tk, hd * LANES:(hd + 1) * LANES]
        qt = qt_ref[0, hd * LANES:(hd + 1) * LANES, qi * tq:(qi + 1) * tq]
        return jnp.dot(k, qt, preferred_element_type=F32)

    pending = {}
    for r in range(-depth, len(rounds)):
        if r + depth < len(rounds):
            for hd in heads:
                pending[hd, r + depth] = scores(hd, *rounds[r + depth])
        if r < 0:
            continue
        qi, n = rounds[r]
        if n == 0:
            m = [jnp.full((1, tq), -jnp.inf, F32) for _ in heads]
            acc = [jnp.zeros((_VT_ROWS, tq), F32) for _ in heads]
        _, vt_ref, j = chunks[n]
        pts, alphas = [], []
        for hd in heads:
            st = pending.pop((hd, r))
            m_new = jnp.maximum(m[hd], jnp.max(st, axis=0, keepdims=True))
            alphas.append(jnp.exp2(m[hd] - m_new))
            pts.append(jnp.exp2(st - m_new).astype(BF16))
            m[hd] = m_new
        for hd in heads:
            vt = vt_ref[0, hd * _VT_ROWS:(hd + 1) * _VT_ROWS, j * tk:(j + 1) * tk]
            acc[hd] = alphas[hd] * acc[hd] + jnp.dot(vt, pts[hd], preferred_element_type=F32)
        if n == nk - 1:
            outs = [a[:MLA_V_DIM] / a[MLA_V_DIM:MLA_V_DIM + 1] for a in acc]
            o_ref[0, qi * tq:(qi + 1) * tq, :] = jnp.concatenate(outs, axis=0).T.astype(o_ref.dtype)


def _attn_call(qt, *kv):
    bsz, _, seq = qt.shape
    tq = tk = 256
    tqs = min(4 * tq, seq)
    hps = 2
    in_specs = [pl.BlockSpec((1, hps * LANES, tqs), lambda b, h, i: (b, h, i))]
    for km, vt in zip(kv[::2], kv[1::2]):
        lk = km.shape[1]
        in_specs += [pl.BlockSpec((1, lk, hps * LANES), lambda b, h, i: (b, 0, h)),
                     pl.BlockSpec((1, hps * _VT_ROWS, lk), lambda b, h, i: (b, h, 0))]
    return pl.pallas_call(
        functools.partial(_attn_kernel, tq=tq, tk=tk, depth=2),
        grid=(bsz, MLA_HEADS // hps, seq // tqs),
        in_specs=in_specs,
        out_specs=pl.BlockSpec((1, tqs, hps * MLA_V_DIM), lambda b, h, i: (b, i, h)),
        out_shape=jax.ShapeDtypeStruct((bsz, seq, MLA_OUT_WIDTH), BF16),
        compiler_params=_cparams(("parallel", "parallel", "parallel")),
        name="attention",
    )(qt, *kv)


def _merge_kernel(x_ref, rq_ref, rkt_ref, rv_ref, sfs_ref, sbs_ref, dm_ref, qs_ref,
                  rg_ref, at_ref, gr_ref, gm_ref, gn_ref, g2_ref,
                  wr_ref, wm_ref, wo_ref, o_ref, a_scr):
    c, dk, dv = RET_CHUNK, RET_QK_DIM, RET_V_DIM
    units = [(ci, h) for ci in range(x_ref.shape[1] // c) for h in range(RET_HEADS)]
    lane = lax.broadcasted_iota(jnp.int32, (c, 2 * dk), 1)
    low = lane < dk

    def q_pair(ci, h):
        g = h // 2
        return rq_ref[0, ci * c:(ci + 1) * c, g * 2 * dk:(g + 1) * 2 * dk]

    scores = {}
    for ci, h in units:
        mine = low if h % 2 == 0 else jnp.logical_not(low)
        qm = jnp.where(mine, q_pair(ci, h), jnp.zeros((), BF16))
        g = h // 2
        kt = rkt_ref[0, g * 2 * dk:(g + 1) * 2 * dk, ci * c:(ci + 1) * c]
        scores[ci, h] = jnp.dot(qm, kt, preferred_element_type=F32)
    outs = {}
    for ci, h in units:
        p = (scores.pop((ci, h)) * dm_ref[h]).astype(BF16)
        qf = q_pair(ci, h).astype(F32)
        qr = pltpu.roll(qf, dk, 1)
        both = jnp.where(low, qf, qr) if h % 2 == 0 else jnp.where(low, qr, qf)
        lhs = jnp.concatenate([p, (both * qs_ref[h]).astype(BF16)], axis=1)
        rhs = jnp.concatenate([rv_ref[0, ci * c:(ci + 1) * c, h * dv:(h + 1) * dv],
                               sfs_ref[0, ci, h * dk:(h + 1) * dk, :],
                               sbs_ref[0, ci, h * dk:(h + 1) * dk, :]], axis=0)
        outs[ci, h] = jnp.dot(lhs, rhs, preferred_element_type=F32)
    mla_l = jnp.dot(at_ref[0], wm_ref[...], preferred_element_type=F32)
    for ci, h in units:
        rows, sl = slice(ci * c, (ci + 1) * c), slice(h * dv, (h + 1) * dv)
        o = outs.pop((ci, h))
        d = o - jnp.mean(o, axis=-1, keepdims=True)
        nrm = d * lax.rsqrt(jnp.mean(d * d, axis=-1, keepdims=True) + EPS) * gn_ref[:, sl]
        a_scr[rows, sl] = (_silu(rg_ref[0, rows, sl].astype(F32)) * nrm).astype(BF16)
    ret_l = jnp.dot(a_scr[...], wr_ref[...], preferred_element_type=F32)
    mrg = _sigmoid(gr_ref[0].astype(F32)) * ret_l + _sigmoid(gm_ref[0].astype(F32)) * mla_l
    out = jnp.dot(mrg.astype(BF16), wo_ref[...], preferred_element_type=F32)
    o_ref[0] = x_ref[0] + g2_ref[0] * out


def _merge_call(x, rq, rkt, rv, sfs, sbs, dm, qs, rg, att, gr, gm, gn, mod, k_gate, wr, wm, wo):
    bsz, seq, d = x.shape
    tm = min(512, seq)
    nch = tm // RET_CHUNK
    row = lambda w_: pl.BlockSpec((1, tm, w_), lambda b, i: (b, i, 0))
    state = pl.BlockSpec((1, nch, RET_QK_WIDTH, RET_V_DIM), lambda b, i: (b, i, 0, 0))
    return pl.pallas_call(
        _merge_kernel,
        grid=(bsz, seq // tm),
        in_specs=[row(d), row(RET_QK_WIDTH), pl.BlockSpec((1, RET_QK_WIDTH, tm), lambda b, i: (b, 0, i)),
                  row(RET_V_WIDTH), state, state, _const_spec(dm.shape), _const_spec(qs.shape),
                  row(RET_V_WIDTH), row(MLA_OUT_WIDTH),
                  row(RET_V_WIDTH), row(RET_V_WIDTH), _const_spec((1, RET_V_WIDTH)), _mod_spec(d, k_gate),
                  _const_spec(wr.shape), _const_spec(wm.shape), _const_spec(wo.shape)],
        out_specs=row(d),
        out_shape=jax.ShapeDtypeStruct(x.shape, F32),
        scratch_shapes=[pltpu.VMEM((tm, RET_V_WIDTH), BF16)],
        compiler_params=_cparams(("parallel", "parallel")),
        name="merge",
    )(x, rq, rkt, rv, sfs, sbs, dm, qs, rg, att, gr, gm, gn.reshape(1, -1), mod, wr, wm, wo)


def _rope_tables(length, n_freq):
    rows = length // GRID_W
    row = jnp.repeat(jnp.arange(rows, dtype=F32), GRID_W)
    col = jnp.tile(jnp.arange(GRID_W, dtype=F32), rows)
    inv_freq = jnp.power(ROPE_BASE, -jnp.arange(n_freq, dtype=F32) / n_freq)
    ang_r = row[:, None] * inv_freq[None, :]
    ang_c = col[:, None] * inv_freq[None, :]
    return jnp.cos(ang_r), jnp.sin(ang_r), jnp.cos(ang_c), jnp.sin(ang_c)


def _pair_tables(t):
    cos_r, sin_r, cos_c, sin_c = t
    return (jnp.concatenate([cos_r, cos_r, cos_c, cos_c], axis=-1),
            jnp.concatenate([-sin_r, sin_r, -sin_c, sin_c], axis=-1))


def _tables(length, rope):
    nr, nm = RET_QK_DIM // 4, MLA_ROPE_DIM // 4
    if rope:
        rt, mt = _rope_tables(length, nr), _rope_tables(length, nm)
    else:
        one, zero = jnp.ones((length, 1), F32), jnp.zeros((length, 1), F32)
        rt = tuple(jnp.tile(v, (1, nr)) for v in (one, zero, one, zero))
        mt = tuple(jnp.tile(v, (1, nm)) for v in (one, zero, one, zero))
    rc, rs = _pair_tables(rt)
    mc, ms = _pair_tables(mt)
    reps = LANES // RET_QK_DIM
    tail = LANES - MLA_NOPE_DIM - MLA_ROPE_DIM
    mcos = jnp.concatenate([jnp.ones((length, MLA_NOPE_DIM), F32), mc, jnp.ones((length, tail), F32)], axis=-1)
    msin = jnp.concatenate([jnp.zeros((length, MLA_NOPE_DIM), F32), ms, jnp.zeros((length, tail), F32)], axis=-1)
    mtab = jnp.concatenate(mt, axis=-1).T
    rtab = jnp.concatenate(rt, axis=-1).T
    return jnp.tile(rc, (1, reps)), jnp.tile(rs, (1, reps)), mcos, msin, mtab, rtab


def _prep_w_in(w_in):
    o_kr = _O_KR
    lead = w_in[:, :o_kr].astype(BF16)
    kr = jnp.pad(w_in[:, o_kr:o_kr + MLA_ROPE_DIM].astype(BF16),
                 ((0, 0), (MLA_NOPE_DIM, LANES - MLA_NOPE_DIM - MLA_ROPE_DIM)))
    gates = w_in[:, o_kr + MLA_ROPE_DIM:].astype(BF16)
    wkt = w_in[:, _O_RK:_O_RV].T.astype(BF16)
    return lead, kr, gates, wkt


def _prep_w_uq(w_uq):
    r = w_uq.shape[0]
    w = w_uq.reshape(r, MLA_HEADS, MLA_QK_DIM)
    w = jnp.pad(w, ((0, 0), (0, 0), (0, LANES - MLA_QK_DIM)))
    return w.reshape(r, MLA_HEADS * LANES).T.astype(BF16)


def _prep_w_ukv(w_ukv):
    r = w_ukv.shape[0]
    w = w_ukv.reshape(r, MLA_HEADS, MLA_NOPE_DIM + MLA_V_DIM)
    wk = jnp.pad(w[..., :MLA_NOPE_DIM], ((0, 0), (0, 0), (0, LANES - MLA_NOPE_DIM)))
    wv = w[..., MLA_NOPE_DIM:]
    return wk.reshape(r, MLA_HEADS * LANES).astype(BF16), wv.reshape(r, MLA_OUT_WIDTH).T.astype(BF16)


def kernel(x, c, ctx, c_ctx, w_ada, b_ada, ffn1_w1, ffn1_w3, ffn1_w2, ffn2_w1, ffn2_w3, ffn2_w2, w_in,
           ret_decay_fwd, ret_decay_bwd, ret_gn, mla_q_norm, mla_kv_norm, w_uq, w_ukv, w_ret_out,
           w_mla_out, w_o, final_norm):
    bsz, seq, d = x.shape
    lc = ctx.shape[1]
    depth = w_ada.shape[0]

    rows = -(-(bsz + 1) // 8) * 8
    cc = jnp.zeros((rows, d), F32).at[:bsz].set(c).at[bsz].set(c_ctx)
    mod_all = _mod_call(cc, w_ada, b_ada)

    tab_lat = _tables(seq, True)
    tab_ctx = _tables(lc, False)
    zero_state = jnp.zeros((bsz, RET_HEADS, RET_QK_DIM, RET_V_DIM), F32)

    xc = ctx
    for l in range(depth):
        last = l == depth - 1
        mod = mod_all[l, :bsz][:, None, :]
        mod_c = jnp.broadcast_to(mod_all[l, bsz][None, None, :], (bsz, 1, N_MOD * d))
        f1 = (ffn1_w1[l].astype(BF16), ffn1_w3[l].astype(BF16), ffn1_w2[l].astype(BF16))
        f2 = (ffn2_w1[l].astype(BF16), ffn2_w3[l].astype(BF16), ffn2_w2[l].astype(BF16))
        w_lead, w_kr, w_gates, wkt_l = _prep_w_in(w_in[l])
        wuqt_l = _prep_w_uq(w_uq[l])
        wuk_l, wvt_l = _prep_w_ukv(w_ukv[l])
        wr, wm, wo = w_ret_out[l].astype(BF16), w_mla_out[l].astype(BF16), w_o[l].astype(BF16)

        x = _ffn_call(x, mod, 0, *f1)
        xc = _ffn_call(xc, mod_c, 0, *f1)

        mla_w = (w_lead, w_kr, w_gates, mla_q_norm[l], mla_kv_norm[l], wuqt_l, wuk_l, wvt_l, wkt_l)
        rq, rkt, rv, rg, qt, km, vt, gr, gm = _inproj_call(x, mod, 3, *mla_w, tab_lat)
        crq, crkt, crv, crg, cqt, ckm, cvt, cgr, cgm = _inproj_call(xc, mod_c, 3, *mla_w, tab_ctx)

        dec = (ret_decay_fwd[l], ret_decay_bwd[l])
        csf, csb, sc_f, sc_b, dm, qs = _ret_state_call(crkt, crv, *dec, zero_state, zero_state)
        sfs, sbs, _, _, _, _ = _ret_state_call(rkt, rv, *dec, sc_f, sc_b)

        att = _attn_call(qt, km, vt, ckm, cvt)
        x = _merge_call(x, rq, rkt, rv, sfs, sbs, dm, qs, rg, att, gr, gm, ret_gn[l], mod, 5, wr, wm, wo)
        x = _ffn_call(x, mod, 6, *f2, final_gain=final_norm if last else None)
        if not last:
            att_c = _attn_call(cqt, ckm, cvt)
            xc = _merge_call(xc, crq, crkt, crv, csf, csb, dm, qs, crg, att_c, cgr, cgm, ret_gn[l],
                             mod_c, 5, wr, wm, wo)
            xc = _ffn_call(xc, mod_c, 6, *f2)
    return x
```

```python
import functools

import jax
import jax.numpy as jnp
import numpy as np
from jax import lax
from jax.experimental import pallas as pl
from jax.experimental.pallas import tpu as pltpu

F32 = jnp.float32
BF16 = jnp.bfloat16

GRID_W = 64
EPS = 1e-6
ROPE_BASE = 10000.0
N_MOD = 9

RET_HEADS = 8
RET_QK_DIM = 64
RET_V_DIM = 128
RET_CHUNK = 128
RET_QK_WIDTH = RET_HEADS * RET_QK_DIM
RET_V_WIDTH = RET_HEADS * RET_V_DIM
RET_SCALE = RET_QK_DIM ** -0.5

MLA_HEADS = 8
MLA_Q_RANK = 384
MLA_KV_RANK = 256
MLA_NOPE_DIM = 64
MLA_ROPE_DIM = 32
MLA_V_DIM = 64
MLA_QK_DIM = MLA_NOPE_DIM + MLA_ROPE_DIM
MLA_OUT_WIDTH = MLA_HEADS * MLA_V_DIM
MLA_SCALE = MLA_QK_DIM ** -0.5

LANES = 128
MLA_HEAD_PAD = LANES
VMEM_LIMIT = 56 * 1024 * 1024
BF16_SUBLANES = 16
_VT_ROWS = MLA_V_DIM + BF16_SUBLANES
_Q_SCALE = MLA_SCALE * 1.4426950408889634
_NT = (((1,), (1,)), ((), ()))

_O_RQ = 0
_O_RK = _O_RQ + RET_QK_WIDTH
_O_RV = _O_RK + RET_QK_WIDTH
_O_RG = _O_RV + RET_V_WIDTH
_O_DQ = _O_RG + RET_V_WIDTH
_O_DKV = _O_DQ + MLA_Q_RANK
_O_KR = _O_DKV + MLA_KV_RANK


def _cparams(sem, vmem=VMEM_LIMIT):
    return pltpu.CompilerParams(dimension_semantics=sem, vmem_limit_bytes=vmem)


def _const_spec(shape):
    n = len(shape)
    return pl.BlockSpec(shape, lambda *_: (0,) * n, pipeline_mode=pl.Buffered(1))


def _rms(xf):
    return xf * lax.rsqrt(jnp.mean(xf * xf, axis=-1, keepdims=True) + EPS)


def _sigmoid(a):
    return 0.5 * jnp.tanh(0.5 * a) + 0.5


def _silu(a):
    return a * _sigmoid(a)


def _mod_kernel(c_ref, w_ref, b_ref, o_ref):
    s = _silu(c_ref[...])
    o_ref[0] = jnp.dot(s, w_ref[0], preferred_element_type=F32,
                       precision=lax.Precision.HIGHEST) + b_ref[0]


def _mod_call(cc, w_ada, b_ada):
    depth, d, nd = w_ada.shape
    rows = cc.shape[0]
    tn = 1024
    return pl.pallas_call(
        _mod_kernel,
        grid=(depth, nd // tn),
        in_specs=[
            pl.BlockSpec((rows, d), lambda l, j: (0, 0)),
            pl.BlockSpec((1, d, tn), lambda l, j: (l, 0, j)),
            pl.BlockSpec((1, 1, tn), lambda l, j: (l, 0, j)),
        ],
        out_specs=pl.BlockSpec((1, rows, tn), lambda l, j: (l, 0, j)),
        out_shape=jax.ShapeDtypeStruct((depth, rows, nd), F32),
        compiler_params=_cparams(("arbitrary", "arbitrary")),
        name="mod",
    )(cc, w_ada, b_ada.reshape(depth, 1, nd))


def _ffn_kernel(x_ref, sh_ref, sc_ref, g_ref, w1_ref, w3_ref, w2_ref, *rest, final):
    if final:
        fn_ref, o_ref = rest
    else:
        (o_ref,) = rest
    xf = x_ref[0]
    h = _rms(xf) * (1.0 + sc_ref[0]) + sh_ref[0]
    hb = h.astype(BF16)
    a = jnp.dot(hb, w1_ref[...], preferred_element_type=F32)
    b = jnp.dot(hb, w3_ref[...], preferred_element_type=F32)
    u = (_silu(a) * b).astype(BF16)
    y = jnp.dot(u, w2_ref[...], preferred_element_type=F32)
    xn = xf + (0.5 * g_ref[0]) * y
    if final:
        xn = _rms(xn) * fn_ref[...]
    o_ref[0] = xn


def _mod_spec(d, k):
    return pl.BlockSpec((1, 1, d), lambda b, i: (b, 0, k))


def _ffn_call(x, mod, k0, w1, w3, w2, final_gain=None):
    bsz, seq, d = x.shape
    dff = w1.shape[1]
    tm = min(512, seq)
    final = final_gain is not None
    in_specs = [
        pl.BlockSpec((1, tm, d), lambda b, i: (b, i, 0)),
        _mod_spec(d, k0), _mod_spec(d, k0 + 1), _mod_spec(d, k0 + 2),
        _const_spec((d, dff)), _const_spec((d, dff)), _const_spec((dff, d)),
    ]
    args = [x, mod, mod, mod, w1, w3, w2]
    if final:
        in_specs.append(_const_spec((1, d)))
        args.append(final_gain.reshape(1, d))
    return pl.pallas_call(
        functools.partial(_ffn_kernel, final=final),
        grid=(bsz, seq // tm),
        in_specs=in_specs,
        out_specs=pl.BlockSpec((1, tm, d), lambda b, i: (b, i, 0)),
        out_shape=jax.ShapeDtypeStruct(x.shape, F32),
        compiler_params=_cparams(("parallel", "parallel")),
        name="ffn_final" if final else "ffn",
    )(*args)


def _rope(y, cos, sin, first_half, off):
    partner = jnp.where(first_half, pltpu.roll(y, LANES - off, 1), pltpu.roll(y, off, 1))
    return y * cos + partner * sin


def _inproj_kernel(x_ref, sh_ref, sc_ref, w_ref, wkr_ref, wg_ref, qn_ref, kvn_ref,
                   wuqt_ref, wuk_ref, wvt_ref, wkt_ref,
                   rcos_ref, rsin_ref, mcos_ref, msin_ref, mtab_ref, rtab_ref,
                   rq_ref, rk_ref, rv_ref, rg_ref, qt_ref, km_ref, vt_ref, gr_ref, gm_ref):
    tm = x_ref.shape[1]
    xf = x_ref[0]
    hb = (_rms(xf) * (1.0 + sc_ref[0]) + sh_ref[0]).astype(BF16)

    def proj(a, b):
        return jnp.dot(hb, w_ref[:, a:b], preferred_element_type=F32)

    lane = lax.broadcasted_iota(jnp.int32, (tm, LANES), 1)
    ret_first = (lane % (RET_QK_DIM // 2)) < (RET_QK_DIM // 4)
    mla_first = (lane % (MLA_ROPE_DIM // 2)) < (MLA_ROPE_DIM // 4)
    rcos, rsin = rcos_ref[...], rsin_ref[...]

    dq_raw = proj(_O_DQ, _O_DKV)
    dkv_raw = proj(_O_DKV, _O_KR)
    kr_raw = jnp.dot(hb, wkr_ref[...], preferred_element_type=F32)
    q = proj(_O_RQ, _O_RK)
    kt = lax.dot_general(wkt_ref[...], hb, _NT, preferred_element_type=F32) * RET_SCALE
    rv_ref[0] = proj(_O_RV, _O_RG).astype(BF16)
    rg_ref[0] = proj(_O_RG, _O_DQ).astype(BF16)
    dq = (_rms(dq_raw) * qn_ref[...]).astype(BF16)
    dkv = (_rms(dkv_raw) * kvn_ref[...]).astype(BF16)
    qt = lax.dot_general(wuqt_ref[...], dq, _NT, preferred_element_type=F32) * _Q_SCALE
    kn = jnp.dot(dkv, wuk_ref[...], preferred_element_type=F32)
    vt = lax.dot_general(wvt_ref[...], dkv, _NT, preferred_element_type=F32)
    gr_ref[0] = jnp.dot(hb, wg_ref[:, :RET_V_WIDTH], preferred_element_type=F32).astype(BF16)
    gm_ref[0] = jnp.dot(hb, wg_ref[:, RET_V_WIDTH:], preferred_element_type=F32).astype(BF16)

    for g in range(RET_QK_WIDTH // LANES):
        sl = slice(g * LANES, (g + 1) * LANES)
        rq_ref[0, :, sl] = _rope(q[:, sl], rcos, rsin, ret_first, RET_QK_DIM // 4).astype(BF16)

    nr = RET_QK_DIM // 4
    rcos_r, rsin_r = rtab_ref[0:nr, :], rtab_ref[nr:2 * nr, :]
    rcos_c, rsin_c = rtab_ref[2 * nr:3 * nr, :], rtab_ref[3 * nr:4 * nr, :]
    for h in range(RET_HEADS):
        r0 = h * RET_QK_DIM
        a, b = kt[r0:r0 + nr], kt[r0 + nr:r0 + 2 * nr]
        c, d = kt[r0 + 2 * nr:r0 + 3 * nr], kt[r0 + 3 * nr:r0 + 4 * nr]
        head = jnp.concatenate([a * rcos_r - b * rsin_r, b * rcos_r + a * rsin_r,
                                c * rcos_c - d * rsin_c, d * rcos_c + c * rsin_c], axis=0)
        rk_ref[0, r0:r0 + RET_QK_DIM, :] = head.astype(BF16)

    nf = MLA_ROPE_DIM // 4
    cos_r, sin_r = mtab_ref[0:nf, :], mtab_ref[nf:2 * nf, :]
    cos_c, sin_c = mtab_ref[2 * nf:3 * nf, :], mtab_ref[3 * nf:4 * nf, :]
    for h in range(MLA_HEADS):
        r0 = h * LANES + MLA_NOPE_DIM
        a, b = qt[r0:r0 + nf], qt[r0 + nf:r0 + 2 * nf]
        c, d = qt[r0 + 2 * nf:r0 + 3 * nf], qt[r0 + 3 * nf:r0 + 4 * nf]
        head = jnp.concatenate([
            qt[h * LANES:r0],
            a * cos_r - b * sin_r, b * cos_r + a * sin_r,
            c * cos_c - d * sin_c, d * cos_c + c * sin_c,
            qt[r0 + MLA_ROPE_DIM:(h + 1) * LANES]], axis=0)
        qt_ref[0, h * LANES:(h + 1) * LANES, :] = head.astype(BF16)

    kr = _rope(kr_raw, mcos_ref[...], msin_ref[...], mla_first, MLA_ROPE_DIM // 4)
    for h in range(MLA_HEADS):
        sl = slice(h * LANES, (h + 1) * LANES)
        km_ref[0, :, sl] = (kn[:, sl] + kr).astype(BF16)
    pad = _VT_ROWS - MLA_V_DIM
    ones_row = (lax.broadcasted_iota(jnp.int32, (pad, tm), 0) == 0).astype(BF16)
    for h in range(MLA_HEADS):
        vt_ref[0, h * _VT_ROWS:h * _VT_ROWS + MLA_V_DIM, :] = vt[h * MLA_V_DIM:(h + 1) * MLA_V_DIM].astype(BF16)
        vt_ref[0, h * _VT_ROWS + MLA_V_DIM:(h + 1) * _VT_ROWS, :] = ones_row


def _inproj_call(x, mod, k0, w, wkr, wg, q_norm, kv_norm, wuqt, wuk, wvt, wkt, tables):
    bsz, seq, d = x.shape
    tm = min(512, seq)
    row = lambda w_: pl.BlockSpec((1, tm, w_), lambda b, i: (b, i, 0))
    col = lambda r_: pl.BlockSpec((1, r_, tm), lambda b, i: (b, 0, i))
    tab = pl.BlockSpec((tm, LANES), lambda b, i: (i, 0))
    ttab = lambda r_: pl.BlockSpec((r_, tm), lambda b, i: (0, i))
    n_q, n_vt = MLA_HEADS * LANES, MLA_HEADS * _VT_ROWS
    shapes = [(seq, RET_QK_WIDTH), (RET_QK_WIDTH, seq), (seq, RET_V_WIDTH), (seq, RET_V_WIDTH),
              (n_q, seq), (seq, n_q), (n_vt, seq), (seq, RET_V_WIDTH), (seq, RET_V_WIDTH)]
    dtypes = (BF16,) * len(shapes)
    out_specs = [col(s[0]) if i in (1, 4, 6) else row(s[1]) for i, s in enumerate(shapes)]
    return pl.pallas_call(
        _inproj_kernel,
        grid=(bsz, seq // tm),
        in_specs=[
            row(d), _mod_spec(d, k0), _mod_spec(d, k0 + 1),
            _const_spec(w.shape), _const_spec(wkr.shape), _const_spec(wg.shape),
            _const_spec((1, MLA_Q_RANK)), _const_spec((1, MLA_KV_RANK)),
            _const_spec(wuqt.shape), _const_spec(wuk.shape), _const_spec(wvt.shape), _const_spec(wkt.shape),
            tab, tab, tab, tab, ttab(MLA_ROPE_DIM), ttab(RET_QK_DIM),
        ],
        out_specs=out_specs,
        out_shape=[jax.ShapeDtypeStruct((bsz,) + s, dt) for s, dt in zip(shapes, dtypes)],
        compiler_params=_cparams(("parallel", "parallel")),
        name="inproj",
    )(x, mod, mod, w, wkr, wg, q_norm.reshape(1, -1), kv_norm.reshape(1, -1), wuqt, wuk, wvt, wkt, *tables)


def _ret_state_kernel(kf_ref, vf_ref, kb_ref, vb_ref, decf_ref, decb_ref, sf0_ref, sb0_ref,
                      sfs_ref, sbs_ref, sf_ref, sb_ref, dm_ref, qs_ref,
                      st_f, st_b, ks_f, ks_b, cd_f, cd_b, *, group):
    b, s = pl.program_id(0), pl.program_id(1)
    c, dk, dv = RET_CHUNK, RET_QK_DIM, RET_V_DIM

    @pl.when(s == 0)
    def _():
        st_f[...] = sf0_ref[0]
        st_b[...] = sb0_ref[0]
        j = lax.broadcasted_iota(jnp.int32, (1, c), 1).astype(F32)
        for h in range(RET_HEADS):
            lgf = -jnp.exp(decf_ref[h])
            lgb = -jnp.exp(decb_ref[h])
            ks_f[h] = jnp.exp(lgf * (c - 1.0 - j))
            ks_b[h] = jnp.exp(lgb * j)
            cd_f[h] = jnp.broadcast_to(jnp.exp(lgf * c), (1, dv))
            cd_b[h] = jnp.broadcast_to(jnp.exp(lgb * c), (1, dv))

    @pl.when((b == 0) & (s == 0))
    def _():
        ii = lax.broadcasted_iota(jnp.int32, (c, c), 0).astype(F32)
        jj = lax.broadcasted_iota(jnp.int32, (c, c), 1).astype(F32)
        diff = ii - jj
        row = lax.broadcasted_iota(jnp.int32, (c, 2 * dk), 0).astype(F32)
        fwd_half = lax.broadcasted_iota(jnp.int32, (c, 2 * dk), 1) < dk
        for h in range(RET_HEADS):
            lgf = -jnp.exp(decf_ref[h])
            lgb = -jnp.exp(decb_ref[h])
            dm_ref[h] = (jnp.where(diff >= 0, jnp.exp(lgf * jnp.maximum(diff, 0.0)), 0.0)
                         + jnp.where(diff <= 0, jnp.exp(lgb * jnp.maximum(-diff, 0.0)), 0.0))
            qs_ref[h] = jnp.exp(jnp.where(fwd_half, lgf * (row + 1.0), lgb * (c - row)))

    kv = {}
    for g in range(group):
        gb = group - 1 - g
        for h in range(RET_HEADS):
            kf = (kf_ref[0, h * dk:(h + 1) * dk, g * c:(g + 1) * c].astype(F32) * ks_f[h]).astype(BF16)
            kv["f", g, h] = jnp.dot(kf, vf_ref[0, g * c:(g + 1) * c, h * dv:(h + 1) * dv],
                                    preferred_element_type=F32)
            kb = (kb_ref[0, h * dk:(h + 1) * dk, gb * c:(gb + 1) * c].astype(F32) * ks_b[h]).astype(BF16)
            kv["b", g, h] = jnp.dot(kb, vb_ref[0, gb * c:(gb + 1) * c, h * dv:(h + 1) * dv],
                                    preferred_element_type=F32)
    for h in range(RET_HEADS):
        sf, sb = st_f[h], st_b[h]
        for g in range(group):
            gb = group - 1 - g
            sfs_ref[0, g, h * dk:(h + 1) * dk, :] = sf.astype(BF16)
            sf = cd_f[h] * sf + kv["f", g, h]
            sbs_ref[0, gb, h * dk:(h + 1) * dk, :] = sb.astype(BF16)
            sb = cd_b[h] * sb + kv["b", g, h]
        st_f[h], st_b[h] = sf, sb

    @pl.when(s == pl.num_programs(1) - 1)
    def _():
        sf_ref[0] = st_f[...]
        sb_ref[0] = st_b[...]


def _ret_state_call(rkt, rv, dec_f, dec_b, s0_f, s0_b):
    bsz, seq, _ = rv.shape
    c = RET_CHUNK
    nc = seq // c
    group = min(8, nc)
    ns = nc // group
    h, dk, dv = RET_HEADS, RET_QK_DIM, RET_V_DIM
    st_spec = pl.BlockSpec((1, h, dk, dv), lambda b, s: (b, 0, 0, 0))
    dec_spec = pl.BlockSpec((h, 1, 1), lambda b, s: (0, 0, 0))
    tab = lambda w_: pl.BlockSpec((h, c, w_), lambda b, s: (0, 0, 0))
    return pl.pallas_call(
        functools.partial(_ret_state_kernel, group=group),
        grid=(bsz, ns),
        in_specs=[
            pl.BlockSpec((1, h * dk, group * c), lambda b, s: (b, 0, s)),
            pl.BlockSpec((1, group * c, h * dv), lambda b, s: (b, s, 0)),
            pl.BlockSpec((1, h * dk, group * c), lambda b, s: (b, 0, ns - 1 - s)),
            pl.BlockSpec((1, group * c, h * dv), lambda b, s: (b, ns - 1 - s, 0)),
            dec_spec, dec_spec, st_spec, st_spec],
        out_specs=[
            pl.BlockSpec((1, group, h * dk, dv), lambda b, s: (b, s, 0, 0)),
            pl.BlockSpec((1, group, h * dk, dv), lambda b, s: (b, ns - 1 - s, 0, 0)),
            st_spec, st_spec, tab(c), tab(2 * dk)],
        out_shape=[jax.ShapeDtypeStruct((bsz, nc, h * dk, dv), BF16)] * 2
        + [jax.ShapeDtypeStruct((bsz, h, dk, dv), F32)] * 2
        + [jax.ShapeDtypeStruct((h, c, c), F32), jax.ShapeDtypeStruct((h, c, 2 * dk), F32)],
        scratch_shapes=[
            pltpu.VMEM((h, dk, dv), F32), pltpu.VMEM((h, dk, dv), F32),
            pltpu.VMEM((h, 1, c), F32), pltpu.VMEM((h, 1, c), F32),
            pltpu.VMEM((h, 1, dv), F32), pltpu.VMEM((h, 1, dv), F32),
        ],
        compiler_params=_cparams(("arbitrary", "arbitrary")),
        name="ret_state",
    )(rkt, rv, rkt, rv, dec_f.reshape(h, 1, 1), dec_b.reshape(h, 1, 1), s0_f, s0_b)


def _attn_kernel(qt_ref, *refs, tq, tk, depth):
    o_ref = refs[-1]
    sources = [(refs[i], refs[i + 1]) for i in range(0, len(refs) - 1, 2)]
    chunks = [(k_ref, vt_ref, j) for k_ref, vt_ref in sources for j in range(k_ref.shape[1] // tk)]
    nk = len(chunks)
    heads = range(qt_ref.shape[1] // LANES)
    rounds = [(qi, n) for qi in range(qt_ref.shape[2] // tq) for n in range(nk)]

    def scores(hd, qi, n):
        k_ref, _, j = chunks[n]
        k = k_ref[0, j * tk:(j + 1) * tk, hd * LANES:(hd + 1) * LANES]
        qt = qt_ref[0, hd * LANES:(hd + 1) * LANES, qi * tq:(qi + 1) * tq]
        return jnp.dot(k, qt, preferred_element_type=F32)

    pending = {}
    for r in range(-depth, len(rounds)):
        if r + depth < len(rounds):
            for hd in heads:
                pending[hd, r + depth] = scores(hd, *rounds[r + depth])
        if r < 0:
            continue
        qi, n = rounds[r]
        if n == 0:
            m = [jnp.full((1, tq), -jnp.inf, F32) for _ in heads]
            acc = [jnp.zeros((_VT_ROWS, tq), F32) for _ in heads]
        _, vt_ref, j = chunks[n]
        pts, alphas = [], []
        for hd in heads:
            st = pending.pop((hd, r))
            m_new = jnp.maximum(m[hd], jnp.max(st, axis=0, keepdims=True))
            alphas.append(jnp.exp2(m[hd] - m_new))
            pts.append(jnp.exp2(st - m_new).astype(BF16))
            m[hd] = m_new
        for hd in heads:
            vt = vt_ref[0, hd * _VT_ROWS:(hd + 1) * _VT_ROWS, j * tk:(j + 1) * tk]
            acc[hd] = alphas[hd] * acc[hd] + jnp.dot(vt, pts[hd], preferred_element_type=F32)
        if n == nk - 1:
            outs = [a[:MLA_V_DIM] / a[MLA_V_DIM:MLA_V_DIM + 1] for a in acc]
            o_ref[0, qi * tq:(qi + 1) * tq, :] = jnp.concatenate(outs, axis=0).T.astype(o_ref.dtype)


def _attn_call(qt, *kv):
    bsz, _, seq = qt.shape
    tq = tk = 256
    tqs = min(4 * tq, seq)
    hps = 2
    in_specs = [pl.BlockSpec((1, hps * LANES, tqs), lambda b, h, i: (b, h, i))]
    for km, vt in zip(kv[::2], kv[1::2]):
        lk = km.shape[1]
        in_specs += [pl.BlockSpec((1, lk, hps * LANES), lambda b, h, i: (b, 0, h)),
                     pl.BlockSpec((1, hps * _VT_ROWS, lk), lambda b, h, i: (b, h, 0))]
    return pl.pallas_call(
        functools.partial(_attn_kernel, tq=tq, tk=tk, depth=2),
        grid=(bsz, MLA_HEADS // hps, seq // tqs),
        in_specs=in_specs,
        out_specs=pl.BlockSpec((1, tqs, hps * MLA_V_DIM), lambda b, h, i: (b, i, h)),
        out_shape=jax.ShapeDtypeStruct((bsz, seq, MLA_OUT_WIDTH), BF16),
        compiler_params=_cparams(("parallel", "parallel", "parallel")),
        name="attention",
    )(qt, *kv)


def _merge_kernel(x_ref, rq_ref, rkt_ref, rv_ref, sfs_ref, sbs_ref, dm_ref, qs_ref,
                  rg_ref, at_ref, gr_ref, gm_ref, gn_ref, g2_ref,
                  wr_ref, wm_ref, wo_ref, o_ref, a_scr):
    c, dk, dv = RET_CHUNK, RET_QK_DIM, RET_V_DIM
    units = [(ci, h) for ci in range(x_ref.shape[1] // c) for h in range(RET_HEADS)]
    lane = lax.broadcasted_iota(jnp.int32, (c, 2 * dk), 1)
    low = lane < dk

    def q_pair(ci, h):
        g = h // 2
        return rq_ref[0, ci * c:(ci + 1) * c, g * 2 * dk:(g + 1) * 2 * dk]

    scores = {}
    for ci, h in units:
        mine = low if h % 2 == 0 else jnp.logical_not(low)
        qm = jnp.where(mine, q_pair(ci, h), jnp.zeros((), BF16))
        g = h // 2
        kt = rkt_ref[0, g * 2 * dk:(g + 1) * 2 * dk, ci * c:(ci + 1) * c]
        scores[ci, h] = jnp.dot(qm, kt, preferred_element_type=F32)
    outs = {}
    for ci, h in units:
        p = (scores.pop((ci, h)) * dm_ref[h]).astype(BF16)
        qf = q_pair(ci, h).astype(F32)
        qr = pltpu.roll(qf, dk, 1)
        both = jnp.where(low, qf, qr) if h % 2 == 0 else jnp.where(low, qr, qf)
        lhs = jnp.concatenate([p, (both * qs_ref[h]).astype(BF16)], axis=1)
        rhs = jnp.concatenate([rv_ref[0, ci * c:(ci + 1) * c, h * dv:(h + 1) * dv],
                               sfs_ref[0, ci, h * dk:(h + 1) * dk, :],
                               sbs_ref[0, ci, h * dk:(h + 1) * dk, :]], axis=0)
        outs[ci, h] = jnp.dot(lhs, rhs, preferred_element_type=F32)
    mla_l = jnp.dot(at_ref[0], wm_ref[...], preferred_element_type=F32)
    for ci, h in units:
        rows, sl = slice(ci * c, (ci + 1) * c), slice(h * dv, (h + 1) * dv)
        o = outs.pop((ci, h))
        d = o - jnp.mean(o, axis=-1, keepdims=True)
        nrm = d * lax.rsqrt(jnp.mean(d * d, axis=-1, keepdims=True) + EPS) * gn_ref[:, sl]
        a_scr[rows, sl] = (_silu(rg_ref[0, rows, sl].astype(F32)) * nrm).astype(BF16)
    ret_l = jnp.dot(a_scr[...], wr_ref[...], preferred_element_type=F32)
    mrg = _sigmoid(gr_ref[0].astype(F32)) * ret_l + _sigmoid(gm_ref[0].astype(F32)) * mla_l
    out = jnp.dot(mrg.astype(BF16), wo_ref[...], preferred_element_type=F32)
    o_ref[0] = x_ref[0] + g2_ref[0] * out


def _merge_call(x, rq, rkt, rv, sfs, sbs, dm, qs, rg, att, gr, gm, gn, mod, k_gate, wr, wm, wo):
    bsz, seq, d = x.shape
    tm = min(512, seq)
    nch = tm // RET_CHUNK
    row = lambda w_: pl.BlockSpec((1, tm, w_), lambda b, i: (b, i, 0))
    state = pl.BlockSpec((1, nch, RET_QK_WIDTH, RET_V_DIM), lambda b, i: (b, i, 0, 0))
    return pl.pallas_call(
        _merge_kernel,
        grid=(bsz, seq // tm),
        in_specs=[row(d), row(RET_QK_WIDTH), pl.BlockSpec((1, RET_QK_WIDTH, tm), lambda b, i: (b, 0, i)),
                  row(RET_V_WIDTH), state, state, _const_spec(dm.shape), _const_spec(qs.shape),
                  row(RET_V_WIDTH), row(MLA_OUT_WIDTH),
                  row(RET_V_WIDTH), row(RET_V_WIDTH), _const_spec((1, RET_V_WIDTH)), _mod_spec(d, k_gate),
                  _const_spec(wr.shape), _const_spec(wm.shape), _const_spec(wo.shape)],
        out_specs=row(d),
        out_shape=jax.ShapeDtypeStruct(x.shape, F32),
        scratch_shapes=[pltpu.VMEM((tm, RET_V_WIDTH), BF16)],
        compiler_params=_cparams(("parallel", "parallel")),
        name="merge",
    )(x, rq, rkt, rv, sfs, sbs, dm, qs, rg, att, gr, gm, gn.reshape(1, -1), mod, wr, wm, wo)


def _rope_tables(length, n_freq):
    f32 = np.float32
    rows = length // GRID_W
    row = np.repeat(np.arange(rows, dtype=f32), GRID_W)
    col = np.tile(np.arange(GRID_W, dtype=f32), rows)
    inv_freq = np.power(f32(ROPE_BASE), -np.arange(n_freq, dtype=f32) / f32(n_freq)).astype(f32)
    ang_r = row[:, None] * inv_freq[None, :]
    ang_c = col[:, None] * inv_freq[None, :]
    return np.cos(ang_r), np.sin(ang_r), np.cos(ang_c), np.sin(ang_c)


def _pair_tables(t):
    cos_r, sin_r, cos_c, sin_c = t
    return (np.concatenate([cos_r, cos_r, cos_c, cos_c], axis=-1),
            np.concatenate([-sin_r, sin_r, -sin_c, sin_c], axis=-1))


def _tables(length, rope):
    f32 = np.float32
    nr, nm = RET_QK_DIM // 4, MLA_ROPE_DIM // 4
    if rope:
        rt, mt = _rope_tables(length, nr), _rope_tables(length, nm)
    else:
        one, zero = np.ones((length, 1), f32), np.zeros((length, 1), f32)
        rt = tuple(np.tile(v, (1, nr)) for v in (one, zero, one, zero))
        mt = tuple(np.tile(v, (1, nm)) for v in (one, zero, one, zero))
    rc, rs = _pair_tables(rt)
    mc, ms = _pair_tables(mt)
    reps = LANES // RET_QK_DIM
    tail = LANES - MLA_NOPE_DIM - MLA_ROPE_DIM
    mcos = np.concatenate([np.ones((length, MLA_NOPE_DIM), f32), mc, np.ones((length, tail), f32)], axis=-1)
    msin = np.concatenate([np.zeros((length, MLA_NOPE_DIM), f32), ms, np.zeros((length, tail), f32)], axis=-1)
    mtab = np.ascontiguousarray(np.concatenate(mt, axis=-1).T)
    rtab = np.ascontiguousarray(np.concatenate(rt, axis=-1).T)
    return np.tile(rc, (1, reps)), np.tile(rs, (1, reps)), mcos, msin, mtab, rtab


def _prep_w_in(w_in, l):
    o_kr = _O_KR
    lead = w_in[l, :, :o_kr].astype(BF16)
    kr = jnp.pad(w_in[l, :, o_kr:o_kr + MLA_ROPE_DIM].astype(BF16),
                 ((0, 0), (MLA_NOPE_DIM, LANES - MLA_NOPE_DIM - MLA_ROPE_DIM)))
    gates = w_in[l, :, o_kr + MLA_ROPE_DIM:].astype(BF16)
    wkt = w_in[l, :, _O_RK:_O_RV].astype(BF16).T
    return lead, kr, gates, wkt


def _prep_w_uq(w_uq):
    r = w_uq.shape[0]
    w = w_uq.reshape(r, MLA_HEADS, MLA_QK_DIM)
    w = jnp.pad(w, ((0, 0), (0, 0), (0, LANES - MLA_QK_DIM)))
    return w.reshape(r, MLA_HEADS * LANES).T.astype(BF16)


def _prep_w_ukv(w_ukv):
    r = w_ukv.shape[0]
    w = w_ukv.reshape(r, MLA_HEADS, MLA_NOPE_DIM + MLA_V_DIM)
    wk = jnp.pad(w[..., :MLA_NOPE_DIM], ((0, 0), (0, 0), (0, LANES - MLA_NOPE_DIM)))
    wv = w[..., MLA_NOPE_DIM:]
    return wk.reshape(r, MLA_HEADS * LANES).astype(BF16), wv.reshape(r, MLA_OUT_WIDTH).T.astype(BF16)


def kernel(x, c, ctx, c_ctx, w_ada, b_ada, ffn1_w1, ffn1_w3, ffn1_w2, ffn2_w1, ffn2_w3, ffn2_w2, w_in,
           ret_decay_fwd, ret_decay_bwd, ret_gn, mla_q_norm, mla_kv_norm, w_uq, w_ukv, w_ret_out,
           w_mla_out, w_o, final_norm):
    bsz, seq, d = x.shape
    lc = ctx.shape[1]
    depth = w_ada.shape[0]

    rows = -(-(bsz + 1) // 8) * 8
    cc = jnp.zeros((rows, d), F32).at[:bsz].set(c).at[bsz].set(c_ctx)
    mod_all = _mod_call(cc, w_ada, b_ada)

    tab_lat = _tables(seq, True)
    tab_ctx = _tables(lc, False)
    zero_state = jnp.zeros((bsz, RET_HEADS, RET_QK_DIM, RET_V_DIM), F32)

    xc = ctx
    for l in range(depth):
        last = l == depth - 1
        mod = mod_all[l, :bsz][:, None, :]
        mod_c = jnp.broadcast_to(mod_all[l, bsz][None, None, :], (bsz, 1, N_MOD * d))
        f1 = (ffn1_w1[l].astype(BF16), ffn1_w3[l].astype(BF16), ffn1_w2[l].astype(BF16))
        f2 = (ffn2_w1[l].astype(BF16), ffn2_w3[l].astype(BF16), ffn2_w2[l].astype(BF16))
        w_lead, w_kr, w_gates, wkt_l = _prep_w_in(w_in, l)
        wuqt_l = _prep_w_uq(w_uq[l])
        wuk_l, wvt_l = _prep_w_ukv(w_ukv[l])
        wr, wm, wo = w_ret_out[l].astype(BF16), w_mla_out[l].astype(BF16), w_o[l].astype(BF16)

        x = _ffn_call(x, mod, 0, *f1)
        xc = _ffn_call(xc, mod_c, 0, *f1)

        mla_w = (w_lead, w_kr, w_gates, mla_q_norm[l], mla_kv_norm[l], wuqt_l, wuk_l, wvt_l, wkt_l)
        rq, rkt, rv, rg, qt, km, vt, gr, gm = _inproj_call(x, mod, 3, *mla_w, tab_lat)
        crq, crkt, crv, crg, cqt, ckm, cvt, cgr, cgm = _inproj_call(xc, mod_c, 3, *mla_w, tab_ctx)

        dec = (ret_decay_fwd[l], ret_decay_bwd[l])
        csf, csb, sc_f, sc_b, dm, qs = _ret_state_call(crkt, crv, *dec, zero_state, zero_state)
        sfs, sbs, _, _, _, _ = _ret_state_call(rkt, rv, *dec, sc_f, sc_b)

        att = _attn_call(qt, km, vt, ckm, cvt)
        x = _merge_call(x, rq, rkt, rv, sfs, sbs, dm, qs, rg, att, gr, gm, ret_gn[l], mod, 5, wr, wm, wo)
        x = _ffn_call(x, mod, 6, *f2, final_gain=final_norm if last else None)
        if not last:
            att_c = _attn_call(cqt, ckm, cvt)
            xc = _merge_call(xc, crq, crkt, crv, csf, csb, dm, qs, crg, att_c, cgr, cgm, ret_gn[l],
                             mod_c, 5, wr, wm, wo)
            xc = _ffn_call(xc, mod_c, 6, *f2)
    return x
```

```python
import functools

import jax
import jax.numpy as jnp
import numpy as np
from jax import lax
from jax.experimental import pallas as pl
from jax.experimental.pallas import tpu as pltpu

F32 = jnp.float32
BF16 = jnp.bfloat16

GRID_W = 64
EPS = 1e-6
ROPE_BASE = 10000.0
N_MOD = 9

RET_HEADS = 8
RET_QK_DIM = 64
RET_V_DIM = 128
RET_CHUNK = 128
RET_QK_WIDTH = RET_HEADS * RET_QK_DIM
RET_V_WIDTH = RET_HEADS * RET_V_DIM
RET_SCALE = RET_QK_DIM ** -0.5

MLA_HEADS = 8
MLA_Q_RANK = 384
MLA_KV_RANK = 256
MLA_NOPE_DIM = 64
MLA_ROPE_DIM = 32
MLA_V_DIM = 64
MLA_QK_DIM = MLA_NOPE_DIM + MLA_ROPE_DIM
MLA_OUT_WIDTH = MLA_HEADS * MLA_V_DIM
MLA_SCALE = MLA_QK_DIM ** -0.5

LANES = 128
MXU_TILE = 256
VMEM_LIMIT = 56 * 1024 * 1024
ROW_TILE = 512
MOD_COL_TILE = 1024
RET_GROUP = 8
ATTN_SUBTILES = 4
ATTN_HEADS_PER_STEP = 2
BF16_SUBLANES = 16
_VT_ROWS = MLA_V_DIM + BF16_SUBLANES
_Q_SCALE = MLA_SCALE * 1.4426950408889634
_NT = (((1,), (1,)), ((), ()))

_O_RQ = 0
_O_RK = _O_RQ + RET_QK_WIDTH
_O_RV = _O_RK + RET_QK_WIDTH
_O_RG = _O_RV + RET_V_WIDTH
_O_DQ = _O_RG + RET_V_WIDTH
_O_DKV = _O_DQ + MLA_Q_RANK
_O_KR = _O_DKV + MLA_KV_RANK


def _cparams(sem, vmem=VMEM_LIMIT):
    return pltpu.CompilerParams(dimension_semantics=sem, vmem_limit_bytes=vmem)


def _const_spec(shape):
    n = len(shape)
    return pl.BlockSpec(shape, lambda *_: (0,) * n, pipeline_mode=pl.Buffered(1))


def _rms(xf):
    return xf * lax.rsqrt(jnp.mean(xf * xf, axis=-1, keepdims=True) + EPS)


def _sigmoid(a):
    return 0.5 * jnp.tanh(0.5 * a) + 0.5


def _silu(a):
    return a * _sigmoid(a)


def _mod_kernel(c_ref, w_ref, b_ref, o_ref):
    s = _silu(c_ref[...])
    o_ref[0] = jnp.dot(s, w_ref[0], preferred_element_type=F32,
                       precision=lax.Precision.HIGHEST) + b_ref[0]


def _mod_call(cc, w_ada, b_ada):
    depth, d, nd = w_ada.shape
    rows = cc.shape[0]
    tn = MOD_COL_TILE
    return pl.pallas_call(
        _mod_kernel,
        grid=(depth, nd // tn),
        in_specs=[
            pl.BlockSpec((rows, d), lambda l, j: (0, 0)),
            pl.BlockSpec((1, d, tn), lambda l, j: (l, 0, j)),
            pl.BlockSpec((1, 1, tn), lambda l, j: (l, 0, j)),
        ],
        out_specs=pl.BlockSpec((1, rows, tn), lambda l, j: (l, 0, j)),
        out_shape=jax.ShapeDtypeStruct((depth, rows, nd), F32),
        compiler_params=_cparams(("arbitrary", "arbitrary")),
        name="mod",
    )(cc, w_ada, b_ada.reshape(depth, 1, nd))


def _ffn_kernel(x_ref, sh_ref, sc_ref, g_ref, w1_ref, w3_ref, w2_ref, *rest, final):
    if final:
        fn_ref, o_ref = rest
    else:
        (o_ref,) = rest
    xf = x_ref[0]
    h = _rms(xf) * (1.0 + sc_ref[0]) + sh_ref[0]
    hb = h.astype(BF16)
    a = jnp.dot(hb, w1_ref[...], preferred_element_type=F32)
    b = jnp.dot(hb, w3_ref[...], preferred_element_type=F32)
    u = (_silu(a) * b).astype(BF16)
    y = jnp.dot(u, w2_ref[...], preferred_element_type=F32)
    xn = xf + (0.5 * g_ref[0]) * y
    if final:
        xn = _rms(xn) * fn_ref[...]
    o_ref[0] = xn


def _mod_spec(d, k):
    return pl.BlockSpec((1, 1, d), lambda b, i: (b, 0, k))


def _ffn_call(x, mod, k0, w1, w3, w2, final_gain=None):
    bsz, seq, d = x.shape
    dff = w1.shape[1]
    tm = min(ROW_TILE, seq)
    final = final_gain is not None
    in_specs = [
        pl.BlockSpec((1, tm, d), lambda b, i: (b, i, 0)),
        _mod_spec(d, k0), _mod_spec(d, k0 + 1), _mod_spec(d, k0 + 2),
        _const_spec((d, dff)), _const_spec((d, dff)), _const_spec((dff, d)),
    ]
    args = [x, mod, mod, mod, w1, w3, w2]
    if final:
        in_specs.append(_const_spec((1, d)))
        args.append(final_gain.reshape(1, d))
    return pl.pallas_call(
        functools.partial(_ffn_kernel, final=final),
        grid=(bsz, seq // tm),
        in_specs=in_specs,
        out_specs=pl.BlockSpec((1, tm, d), lambda b, i: (b, i, 0)),
        out_shape=jax.ShapeDtypeStruct(x.shape, F32),
        compiler_params=_cparams(("parallel", "parallel")),
        name="ffn_final" if final else "ffn",
    )(*args)


def _rope(y, cos, sin, first_half, off):
    partner = jnp.where(first_half, pltpu.roll(y, LANES - off, 1), pltpu.roll(y, off, 1))
    return y * cos + partner * sin


def _inproj_kernel(x_ref, sh_ref, sc_ref, w_ref, wkr_ref, wg_ref, qn_ref, kvn_ref,
                   wuqt_ref, wuk_ref, wvt_ref, wkt_ref,
                   rcos_ref, rsin_ref, mcos_ref, msin_ref, mtab_ref, rtab_ref,
                   rq_ref, rk_ref, rv_ref, rg_ref, qt_ref, km_ref, vt_ref, gr_ref, gm_ref):
    tm = x_ref.shape[1]
    xf = x_ref[0]
    hb = (_rms(xf) * (1.0 + sc_ref[0]) + sh_ref[0]).astype(BF16)

    def proj(a, b):
        return jnp.dot(hb, w_ref[:, a:b], preferred_element_type=F32)

    lane = lax.broadcasted_iota(jnp.int32, (tm, LANES), 1)
    ret_first = (lane % (RET_QK_DIM // 2)) < (RET_QK_DIM // 4)
    mla_first = (lane % (MLA_ROPE_DIM // 2)) < (MLA_ROPE_DIM // 4)
    rcos, rsin = rcos_ref[...], rsin_ref[...]

    dq_raw = proj(_O_DQ, _O_DKV)
    dkv_raw = proj(_O_DKV, _O_KR)
    kr_raw = jnp.dot(hb, wkr_ref[...], preferred_element_type=F32)
    q = proj(_O_RQ, _O_RK)
    kt = lax.dot_general(wkt_ref[...], hb, _NT, preferred_element_type=F32) * RET_SCALE
    rv_ref[0] = proj(_O_RV, _O_RG).astype(BF16)
    rg_ref[0] = proj(_O_RG, _O_DQ).astype(BF16)
    dq = (_rms(dq_raw) * qn_ref[...]).astype(BF16)
    dkv = (_rms(dkv_raw) * kvn_ref[...]).astype(BF16)
    qt = lax.dot_general(wuqt_ref[...], dq, _NT, preferred_element_type=F32) * _Q_SCALE
    kn = jnp.dot(dkv, wuk_ref[...], preferred_element_type=F32)
    vt = lax.dot_general(wvt_ref[...], dkv, _NT, preferred_element_type=F32)
    gr_ref[0] = jnp.dot(hb, wg_ref[:, :RET_V_WIDTH], preferred_element_type=F32).astype(BF16)
    gm_ref[0] = jnp.dot(hb, wg_ref[:, RET_V_WIDTH:], preferred_element_type=F32).astype(BF16)

    for g in range(RET_QK_WIDTH // LANES):
        sl = slice(g * LANES, (g + 1) * LANES)
        rq_ref[0, :, sl] = _rope(q[:, sl], rcos, rsin, ret_first, RET_QK_DIM // 4).astype(BF16)

    nr = RET_QK_DIM // 4
    rcos_r, rsin_r = rtab_ref[0:nr, :], rtab_ref[nr:2 * nr, :]
    rcos_c, rsin_c = rtab_ref[2 * nr:3 * nr, :], rtab_ref[3 * nr:4 * nr, :]
    for h in range(RET_HEADS):
        r0 = h * RET_QK_DIM
        a, b = kt[r0:r0 + nr], kt[r0 + nr:r0 + 2 * nr]
        c, d = kt[r0 + 2 * nr:r0 + 3 * nr], kt[r0 + 3 * nr:r0 + 4 * nr]
        head = jnp.concatenate([a * rcos_r - b * rsin_r, b * rcos_r + a * rsin_r,
                                c * rcos_c - d * rsin_c, d * rcos_c + c * rsin_c], axis=0)
        rk_ref[0, r0:r0 + RET_QK_DIM, :] = head.astype(BF16)

    nf = MLA_ROPE_DIM // 4
    cos_r, sin_r = mtab_ref[0:nf, :], mtab_ref[nf:2 * nf, :]
    cos_c, sin_c = mtab_ref[2 * nf:3 * nf, :], mtab_ref[3 * nf:4 * nf, :]
    for h in range(MLA_HEADS):
        r0 = h * LANES + MLA_NOPE_DIM
        a, b = qt[r0:r0 + nf], qt[r0 + nf:r0 + 2 * nf]
        c, d = qt[r0 + 2 * nf:r0 + 3 * nf], qt[r0 + 3 * nf:r0 + 4 * nf]
        head = jnp.concatenate([
            qt[h * LANES:r0],
            a * cos_r - b * sin_r, b * cos_r + a * sin_r,
            c * cos_c - d * sin_c, d * cos_c + c * sin_c,
            qt[r0 + MLA_ROPE_DIM:(h + 1) * LANES]], axis=0)
        qt_ref[0, h * LANES:(h + 1) * LANES, :] = head.astype(BF16)

    kr = _rope(kr_raw, mcos_ref[...], msin_ref[...], mla_first, MLA_ROPE_DIM // 4)
    for h in range(MLA_HEADS):
        sl = slice(h * LANES, (h + 1) * LANES)
        km_ref[0, :, sl] = (kn[:, sl] + kr).astype(BF16)
    pad = _VT_ROWS - MLA_V_DIM
    ones_row = (lax.broadcasted_iota(jnp.int32, (pad, tm), 0) == 0).astype(BF16)
    for h in range(MLA_HEADS):
        vt_ref[0, h * _VT_ROWS:h * _VT_ROWS + MLA_V_DIM, :] = vt[h * MLA_V_DIM:(h + 1) * MLA_V_DIM].astype(BF16)
        vt_ref[0, h * _VT_ROWS + MLA_V_DIM:(h + 1) * _VT_ROWS, :] = ones_row


def _inproj_call(x, mod, k0, w, wkr, wg, q_norm, kv_norm, wuqt, wuk, wvt, wkt, tables):
    bsz, seq, d = x.shape
    tm = min(ROW_TILE, seq)
    row = lambda w_: pl.BlockSpec((1, tm, w_), lambda b, i: (b, i, 0))
    col = lambda r_: pl.BlockSpec((1, r_, tm), lambda b, i: (b, 0, i))
    tab = pl.BlockSpec((tm, LANES), lambda b, i: (i, 0))
    ttab = lambda r_: pl.BlockSpec((r_, tm), lambda b, i: (0, i))
    n_q, n_vt = MLA_HEADS * LANES, MLA_HEADS * _VT_ROWS
    shapes = [(seq, RET_QK_WIDTH), (RET_QK_WIDTH, seq), (seq, RET_V_WIDTH), (seq, RET_V_WIDTH),
              (n_q, seq), (seq, n_q), (n_vt, seq), (seq, RET_V_WIDTH), (seq, RET_V_WIDTH)]
    dtypes = (BF16,) * len(shapes)
    out_specs = [col(s[0]) if i in (1, 4, 6) else row(s[1]) for i, s in enumerate(shapes)]
    return pl.pallas_call(
        _inproj_kernel,
        grid=(bsz, seq // tm),
        in_specs=[
            row(d), _mod_spec(d, k0), _mod_spec(d, k0 + 1),
            _const_spec(w.shape), _const_spec(wkr.shape), _const_spec(wg.shape),
            _const_spec((1, MLA_Q_RANK)), _const_spec((1, MLA_KV_RANK)),
            _const_spec(wuqt.shape), _const_spec(wuk.shape), _const_spec(wvt.shape), _const_spec(wkt.shape),
            tab, tab, tab, tab, ttab(MLA_ROPE_DIM), ttab(RET_QK_DIM),
        ],
        out_specs=out_specs,
        out_shape=[jax.ShapeDtypeStruct((bsz,) + s, dt) for s, dt in zip(shapes, dtypes)],
        compiler_params=_cparams(("parallel", "parallel")),
        name="inproj",
    )(x, mod, mod, w, wkr, wg, q_norm.reshape(1, -1), kv_norm.reshape(1, -1), wuqt, wuk, wvt, wkt, *tables)


def _ret_state_kernel(kf_ref, vf_ref, kb_ref, vb_ref, decf_ref, decb_ref, sf0_ref, sb0_ref,
                      sfs_ref, sbs_ref, sf_ref, sb_ref, dm_ref, qs_ref,
                      st_f, st_b, ks_f, ks_b, cd_f, cd_b, *, group):
    b, s = pl.program_id(0), pl.program_id(1)
    c, dk, dv = RET_CHUNK, RET_QK_DIM, RET_V_DIM

    @pl.when(s == 0)
    def _():
        st_f[...] = sf0_ref[0]
        st_b[...] = sb0_ref[0]
        j = lax.broadcasted_iota(jnp.int32, (1, c), 1).astype(F32)
        for h in range(RET_HEADS):
            lgf = -jnp.exp(decf_ref[h])
            lgb = -jnp.exp(decb_ref[h])
            ks_f[h] = jnp.exp(lgf * (c - 1.0 - j))
            ks_b[h] = jnp.exp(lgb * j)
            cd_f[h] = jnp.broadcast_to(jnp.exp(lgf * c), (1, dv))
            cd_b[h] = jnp.broadcast_to(jnp.exp(lgb * c), (1, dv))

    @pl.when((b == 0) & (s == 0))
    def _():
        ii = lax.broadcasted_iota(jnp.int32, (c, c), 0).astype(F32)
        jj = lax.broadcasted_iota(jnp.int32, (c, c), 1).astype(F32)
        diff = ii - jj
        row = lax.broadcasted_iota(jnp.int32, (c, 2 * dk), 0).astype(F32)
        fwd_half = lax.broadcasted_iota(jnp.int32, (c, 2 * dk), 1) < dk
        for h in range(RET_HEADS):
            lgf = -jnp.exp(decf_ref[h])
            lgb = -jnp.exp(decb_ref[h])
            dm_ref[h] = (jnp.where(diff >= 0, jnp.exp(lgf * jnp.maximum(diff, 0.0)), 0.0)
                         + jnp.where(diff <= 0, jnp.exp(lgb * jnp.maximum(-diff, 0.0)), 0.0))
            qs_ref[h] = jnp.exp(jnp.where(fwd_half, lgf * (row + 1.0), lgb * (c - row)))

    kv = {}
    for g in range(group):
        gb = group - 1 - g
        for h in range(RET_HEADS):
            kf = (kf_ref[0, h * dk:(h + 1) * dk, g * c:(g + 1) * c].astype(F32) * ks_f[h]).astype(BF16)
            kv["f", g, h] = jnp.dot(kf, vf_ref[0, g * c:(g + 1) * c, h * dv:(h + 1) * dv],
                                    preferred_element_type=F32)
            kb = (kb_ref[0, h * dk:(h + 1) * dk, gb * c:(gb + 1) * c].astype(F32) * ks_b[h]).astype(BF16)
            kv["b", g, h] = jnp.dot(kb, vb_ref[0, gb * c:(gb + 1) * c, h * dv:(h + 1) * dv],
                                    preferred_element_type=F32)
    for h in range(RET_HEADS):
        sf, sb = st_f[h], st_b[h]
        for g in range(group):
            gb = group - 1 - g
            sfs_ref[0, g, h * dk:(h + 1) * dk, :] = sf.astype(BF16)
            sf = cd_f[h] * sf + kv["f", g, h]
            sbs_ref[0, gb, h * dk:(h + 1) * dk, :] = sb.astype(BF16)
            sb = cd_b[h] * sb + kv["b", g, h]
        st_f[h], st_b[h] = sf, sb

    @pl.when(s == pl.num_programs(1) - 1)
    def _():
        sf_ref[0] = st_f[...]
        sb_ref[0] = st_b[...]


def _ret_state_call(rkt, rv, dec_f, dec_b, s0_f, s0_b):
    bsz, seq, _ = rv.shape
    c = RET_CHUNK
    nc = seq // c
    group = min(RET_GROUP, nc)
    ns = nc // group
    h, dk, dv = RET_HEADS, RET_QK_DIM, RET_V_DIM
    st_spec = pl.BlockSpec((1, h, dk, dv), lambda b, s: (b, 0, 0, 0))
    dec_spec = pl.BlockSpec((h, 1, 1), lambda b, s: (0, 0, 0))
    tab = lambda w_: pl.BlockSpec((h, c, w_), lambda b, s: (0, 0, 0))
    return pl.pallas_call(
        functools.partial(_ret_state_kernel, group=group),
        grid=(bsz, ns),
        in_specs=[
            pl.BlockSpec((1, h * dk, group * c), lambda b, s: (b, 0, s)),
            pl.BlockSpec((1, group * c, h * dv), lambda b, s: (b, s, 0)),
            pl.BlockSpec((1, h * dk, group * c), lambda b, s: (b, 0, ns - 1 - s)),
            pl.BlockSpec((1, group * c, h * dv), lambda b, s: (b, ns - 1 - s, 0)),
            dec_spec, dec_spec, st_spec, st_spec],
        out_specs=[
            pl.BlockSpec((1, group, h * dk, dv), lambda b, s: (b, s, 0, 0)),
            pl.BlockSpec((1, group, h * dk, dv), lambda b, s: (b, ns - 1 - s, 0, 0)),
            st_spec, st_spec, tab(c), tab(2 * dk)],
        out_shape=[jax.ShapeDtypeStruct((bsz, nc, h * dk, dv), BF16)] * 2
        + [jax.ShapeDtypeStruct((bsz, h, dk, dv), F32)] * 2
        + [jax.ShapeDtypeStruct((h, c, c), F32), jax.ShapeDtypeStruct((h, c, 2 * dk), F32)],
        scratch_shapes=[
            pltpu.VMEM((h, dk, dv), F32), pltpu.VMEM((h, dk, dv), F32),
            pltpu.VMEM((h, 1, c), F32), pltpu.VMEM((h, 1, c), F32),
            pltpu.VMEM((h, 1, dv), F32), pltpu.VMEM((h, 1, dv), F32),
        ],
        compiler_params=_cparams(("arbitrary", "arbitrary")),
        name="ret_state",
    )(rkt, rv, rkt, rv, dec_f.reshape(h, 1, 1), dec_b.reshape(h, 1, 1), s0_f, s0_b)


def _attn_kernel(qt_ref, *refs, tq, tk, depth):
    o_ref = refs[-1]
    sources = [(refs[i], refs[i + 1]) for i in range(0, len(refs) - 1, 2)]
    chunks = [(k_ref, vt_ref, j) for k_ref, vt_ref in sources for j in range(k_ref.shape[1] // tk)]
    nk = len(chunks)
    heads = range(qt_ref.shape[1] // LANES)
    rounds = [(qi, n) for n in range(nk) for qi in range(qt_ref.shape[2] // tq)]

    def scores(hd, qi, n):
        k_ref, _, j = chunks[n]
        k = k_ref[0, j * tk:(j + 1) * tk, hd * LANES:(hd + 1) * LANES]
        qt = qt_ref[0, hd * LANES:(hd + 1) * LANES, qi * tq:(qi + 1) * tq]
        return jnp.dot(k, qt, preferred_element_type=F32)

    pending, m_all, acc_all = {}, {}, {}
    for r in range(-depth, len(rounds)):
        if r + depth < len(rounds):
            for hd in heads:
                pending[hd, r + depth] = scores(hd, *rounds[r + depth])
        if r < 0:
            continue
        qi, n = rounds[r]
        if n == 0:
            m_all[qi] = [jnp.full((1, tq), -jnp.inf, F32) for _ in heads]
            acc_all[qi] = [jnp.zeros((_VT_ROWS, tq), F32) for _ in heads]
        m, acc = m_all[qi], acc_all[qi]
        _, vt_ref, j = chunks[n]
        pts, alphas = [], []
        for hd in heads:
            st = pending.pop((hd, r))
            m_new = jnp.maximum(m[hd], jnp.max(st, axis=0, keepdims=True))
            alphas.append(jnp.exp2(m[hd] - m_new))
            pts.append(jnp.exp2(st - m_new).astype(BF16))
            m[hd] = m_new
        for hd in heads:
            vt = vt_ref[0, hd * _VT_ROWS:(hd + 1) * _VT_ROWS, j * tk:(j + 1) * tk]
            acc[hd] = alphas[hd] * acc[hd] + jnp.dot(vt, pts[hd], preferred_element_type=F32)
        if n == nk - 1:
            outs = [a[:MLA_V_DIM] / a[MLA_V_DIM:MLA_V_DIM + 1] for a in acc]
            o_ref[0, qi * tq:(qi + 1) * tq, :] = jnp.concatenate(outs, axis=0).T.astype(o_ref.dtype)


def _attn_call(qt, *kv):
    bsz, _, seq = qt.shape
    tq = tk = MXU_TILE
    tqs = min(ATTN_SUBTILES * tq, seq)
    hps = ATTN_HEADS_PER_STEP
    in_specs = [pl.BlockSpec((1, hps * LANES, tqs), lambda b, h, i: (b, h, i))]
    for km, vt in zip(kv[::2], kv[1::2]):
        lk = km.shape[1]
        in_specs += [pl.BlockSpec((1, lk, hps * LANES), lambda b, h, i: (b, 0, h)),
                     pl.BlockSpec((1, hps * _VT_ROWS, lk), lambda b, h, i: (b, h, 0))]
    return pl.pallas_call(
        functools.partial(_attn_kernel, tq=tq, tk=tk, depth=2),
        grid=(bsz, MLA_HEADS // hps, seq // tqs),
        in_specs=in_specs,
        out_specs=pl.BlockSpec((1, tqs, hps * MLA_V_DIM), lambda b, h, i: (b, i, h)),
        out_shape=jax.ShapeDtypeStruct((bsz, seq, MLA_OUT_WIDTH), BF16),
        compiler_params=_cparams(("parallel", "parallel", "parallel")),
        name="attention",
    )(qt, *kv)


def _merge_kernel(x_ref, rq_ref, rkt_ref, rv_ref, sfs_ref, sbs_ref, dm_ref, qs_ref,
                  rg_ref, at_ref, gr_ref, gm_ref, gn_ref, g2_ref,
                  wr_ref, wm_ref, wo_ref, o_ref, a_scr):
    c, dk, dv = RET_CHUNK, RET_QK_DIM, RET_V_DIM
    units = [(ci, h) for ci in range(x_ref.shape[1] // c) for h in range(RET_HEADS)]
    lane = lax.broadcasted_iota(jnp.int32, (c, 2 * dk), 1)
    low = lane < dk

    def q_pair(ci, h):
        g = h // 2
        return rq_ref[0, ci * c:(ci + 1) * c, g * 2 * dk:(g + 1) * 2 * dk]

    scores = {}
    for ci, h in units:
        mine = low if h % 2 == 0 else jnp.logical_not(low)
        qm = jnp.where(mine, q_pair(ci, h), jnp.zeros((), BF16))
        g = h // 2
        kt = rkt_ref[0, g * 2 * dk:(g + 1) * 2 * dk, ci * c:(ci + 1) * c]
        scores[ci, h] = jnp.dot(qm, kt, preferred_element_type=F32)
    outs = {}
    for ci, h in units:
        p = (scores.pop((ci, h)) * dm_ref[h]).astype(BF16)
        qf = q_pair(ci, h).astype(F32)
        qr = pltpu.roll(qf, dk, 1)
        both = jnp.where(low, qf, qr) if h % 2 == 0 else jnp.where(low, qr, qf)
        lhs = jnp.concatenate([p, (both * qs_ref[h]).astype(BF16)], axis=1)
        rhs = jnp.concatenate([rv_ref[0, ci * c:(ci + 1) * c, h * dv:(h + 1) * dv],
                               sfs_ref[0, ci, h * dk:(h + 1) * dk, :],
                               sbs_ref[0, ci, h * dk:(h + 1) * dk, :]], axis=0)
        outs[ci, h] = jnp.dot(lhs, rhs, preferred_element_type=F32)
    mla_l = jnp.dot(at_ref[0], wm_ref[...], preferred_element_type=F32)
    for ci, h in units:
        rows, sl = slice(ci * c, (ci + 1) * c), slice(h * dv, (h + 1) * dv)
        o = outs.pop((ci, h))
        d = o - jnp.mean(o, axis=-1, keepdims=True)
        nrm = d * lax.rsqrt(jnp.mean(d * d, axis=-1, keepdims=True) + EPS) * gn_ref[:, sl]
        a_scr[rows, sl] = (_silu(rg_ref[0, rows, sl].astype(F32)) * nrm).astype(BF16)
    ret_l = jnp.dot(a_scr[...], wr_ref[...], preferred_element_type=F32)
    mrg = _sigmoid(gr_ref[0].astype(F32)) * ret_l + _sigmoid(gm_ref[0].astype(F32)) * mla_l
    out = jnp.dot(mrg.astype(BF16), wo_ref[...], preferred_element_type=F32)
    o_ref[0] = x_ref[0] + g2_ref[0] * out


def _merge_call(x, rq, rkt, rv, sfs, sbs, dm, qs, rg, att, gr, gm, gn, mod, k_gate, wr, wm, wo):
    bsz, seq, d = x.shape
    tm = min(ROW_TILE, seq)
    nch = tm // RET_CHUNK
    row = lambda w_: pl.BlockSpec((1, tm, w_), lambda b, i: (b, i, 0))
    state = pl.BlockSpec((1, nch, RET_QK_WIDTH, RET_V_DIM), lambda b, i: (b, i, 0, 0))
    return pl.pallas_call(
        _merge_kernel,
        grid=(bsz, seq // tm),
        in_specs=[row(d), row(RET_QK_WIDTH), pl.BlockSpec((1, RET_QK_WIDTH, tm), lambda b, i: (b, 0, i)),
                  row(RET_V_WIDTH), state, state, _const_spec(dm.shape), _const_spec(qs.shape),
                  row(RET_V_WIDTH), row(MLA_OUT_WIDTH),
                  row(RET_V_WIDTH), row(RET_V_WIDTH), _const_spec((1, RET_V_WIDTH)), _mod_spec(d, k_gate),
                  _const_spec(wr.shape), _const_spec(wm.shape), _const_spec(wo.shape)],
        out_specs=row(d),
        out_shape=jax.ShapeDtypeStruct(x.shape, F32),
        scratch_shapes=[pltpu.VMEM((tm, RET_V_WIDTH), BF16)],
        compiler_params=_cparams(("parallel", "parallel")),
        name="merge",
    )(x, rq, rkt, rv, sfs, sbs, dm, qs, rg, att, gr, gm, gn.reshape(1, -1), mod, wr, wm, wo)


def _rope_tables(length, n_freq):
    f32 = np.float32
    rows = length // GRID_W
    row = np.repeat(np.arange(rows, dtype=f32), GRID_W)
    col = np.tile(np.arange(GRID_W, dtype=f32), rows)
    inv_freq = np.power(f32(ROPE_BASE), -np.arange(n_freq, dtype=f32) / f32(n_freq)).astype(f32)
    ang_r = row[:, None] * inv_freq[None, :]
    ang_c = col[:, None] * inv_freq[None, :]
    return np.cos(ang_r), np.sin(ang_r), np.cos(ang_c), np.sin(ang_c)


def _pair_tables(t):
    cos_r, sin_r, cos_c, sin_c = t
    return (np.concatenate([cos_r, cos_r, cos_c, cos_c], axis=-1),
            np.concatenate([-sin_r, sin_r, -sin_c, sin_c], axis=-1))


def _tables(length, rope):
    f32 = np.float32
    nr, nm = RET_QK_DIM // 4, MLA_ROPE_DIM // 4
    if rope:
        rt, mt = _rope_tables(length, nr), _rope_tables(length, nm)
    else:
        one, zero = np.ones((length, 1), f32), np.zeros((length, 1), f32)
        rt = tuple(np.tile(v, (1, nr)) for v in (one, zero, one, zero))
        mt = tuple(np.tile(v, (1, nm)) for v in (one, zero, one, zero))
    rc, rs = _pair_tables(rt)
    mc, ms = _pair_tables(mt)
    reps = LANES // RET_QK_DIM
    tail = LANES - MLA_NOPE_DIM - MLA_ROPE_DIM
    mcos = np.concatenate([np.ones((length, MLA_NOPE_DIM), f32), mc, np.ones((length, tail), f32)], axis=-1)
    msin = np.concatenate([np.zeros((length, MLA_NOPE_DIM), f32), ms, np.zeros((length, tail), f32)], axis=-1)
    mtab = np.ascontiguousarray(np.concatenate(mt, axis=-1).T)
    rtab = np.ascontiguousarray(np.concatenate(rt, axis=-1).T)
    return np.tile(rc, (1, reps)), np.tile(rs, (1, reps)), mcos, msin, mtab, rtab


def _prep_w_in(w_in, l):
    o_kr = _O_KR
    lead = w_in[l, :, :o_kr].astype(BF16)
    kr = jnp.pad(w_in[l, :, o_kr:o_kr + MLA_ROPE_DIM].astype(BF16),
                 ((0, 0), (MLA_NOPE_DIM, LANES - MLA_NOPE_DIM - MLA_ROPE_DIM)))
    gates = w_in[l, :, o_kr + MLA_ROPE_DIM:].astype(BF16)
    wkt = w_in[l, :, _O_RK:_O_RV].astype(BF16).T
    return lead, kr, gates, wkt


def _prep_w_uq(w_uq):
    r = w_uq.shape[0]
    w = w_uq.reshape(r, MLA_HEADS, MLA_QK_DIM)
    w = jnp.pad(w, ((0, 0), (0, 0), (0, LANES - MLA_QK_DIM)))
    return w.reshape(r, MLA_HEADS * LANES).T.astype(BF16)


def _prep_w_ukv(w_ukv):
    r = w_ukv.shape[0]
    w = w_ukv.reshape(r, MLA_HEADS, MLA_NOPE_DIM + MLA_V_DIM)
    wk = jnp.pad(w[..., :MLA_NOPE_DIM], ((0, 0), (0, 0), (0, LANES - MLA_NOPE_DIM)))
    wv = w[..., MLA_NOPE_DIM:]
    return wk.reshape(r, MLA_HEADS * LANES).astype(BF16), wv.reshape(r, MLA_OUT_WIDTH).T.astype(BF16)


def kernel(x, c, ctx, c_ctx, w_ada, b_ada, ffn1_w1, ffn1_w3, ffn1_w2, ffn2_w1, ffn2_w3, ffn2_w2, w_in,
           ret_decay_fwd, ret_decay_bwd, ret_gn, mla_q_norm, mla_kv_norm, w_uq, w_ukv, w_ret_out,
           w_mla_out, w_o, final_norm):
    bsz, seq, d = x.shape
    lc = ctx.shape[1]
    depth = w_ada.shape[0]

    rows = -(-(bsz + 1) // 8) * 8
    cc = jnp.zeros((rows, d), F32).at[:bsz].set(c).at[bsz].set(c_ctx)
    mod_all = _mod_call(cc, w_ada, b_ada)

    tab_lat = _tables(seq, True)
    tab_ctx = _tables(lc, False)
    zero_state = jnp.zeros((bsz, RET_HEADS, RET_QK_DIM, RET_V_DIM), F32)

    xc = ctx
    for l in range(depth):
        last = l == depth - 1
        mod = mod_all[l, :bsz][:, None, :]
        mod_c = jnp.broadcast_to(mod_all[l, bsz][None, None, :], (bsz, 1, N_MOD * d))
        f1 = (ffn1_w1[l].astype(BF16), ffn1_w3[l].astype(BF16), ffn1_w2[l].astype(BF16))
        f2 = (ffn2_w1[l].astype(BF16), ffn2_w3[l].astype(BF16), ffn2_w2[l].astype(BF16))
        w_lead, w_kr, w_gates, wkt_l = _prep_w_in(w_in, l)
        wuqt_l = _prep_w_uq(w_uq[l])
        wuk_l, wvt_l = _prep_w_ukv(w_ukv[l])
        wr, wm, wo = w_ret_out[l].astype(BF16), w_mla_out[l].astype(BF16), w_o[l].astype(BF16)

        x = _ffn_call(x, mod, 0, *f1)
        xc = _ffn_call(xc, mod_c, 0, *f1)

        mla_w = (w_lead, w_kr, w_gates, mla_q_norm[l], mla_kv_norm[l], wuqt_l, wuk_l, wvt_l, wkt_l)
        rq, rkt, rv, rg, qt, km, vt, gr, gm = _inproj_call(x, mod, 3, *mla_w, tab_lat)
        crq, crkt, crv, crg, cqt, ckm, cvt, cgr, cgm = _inproj_call(xc, mod_c, 3, *mla_w, tab_ctx)

        dec = (ret_decay_fwd[l], ret_decay_bwd[l])
        csf, csb, sc_f, sc_b, dm, qs = _ret_state_call(crkt, crv, *dec, zero_state, zero_state)
        sfs, sbs, _, _, _, _ = _ret_state_call(rkt, rv, *dec, sc_f, sc_b)

        att = _attn_call(qt, km, vt, ckm, cvt)
        x = _merge_call(x, rq, rkt, rv, sfs, sbs, dm, qs, rg, att, gr, gm, ret_gn[l], mod, 5, wr, wm, wo)
        x = _ffn_call(x, mod, 6, *f2, final_gain=final_norm if last else None)
        if not last:
            att_c = _attn_call(cqt, ckm, cvt)
            xc = _merge_call(xc, crq, crkt, crv, csf, csb, dm, qs, crg, att_c, cgr, cgm, ret_gn[l],
                             mod_c, 5, wr, wm, wo)
            xc = _ffn_call(xc, mod_c, 6, *f2)
    return x
```

```python
import functools

import jax
import jax.numpy as jnp
import numpy as np
from jax import lax
from jax.experimental import pallas as pl
from jax.experimental.pallas import tpu as pltpu

F32 = jnp.float32
BF16 = jnp.bfloat16

GRID_W = 64
EPS = 1e-6
ROPE_BASE = 10000.0
N_MOD = 9

RET_HEADS = 8
RET_QK_DIM = 64
RET_V_DIM = 128
RET_CHUNK = 128
RET_QK_WIDTH = RET_HEADS * RET_QK_DIM
RET_V_WIDTH = RET_HEADS * RET_V_DIM
RET_SCALE = RET_QK_DIM ** -0.5

MLA_HEADS = 8
MLA_Q_RANK = 384
MLA_KV_RANK = 256
MLA_NOPE_DIM = 64
MLA_ROPE_DIM = 32
MLA_V_DIM = 64
MLA_QK_DIM = MLA_NOPE_DIM + MLA_ROPE_DIM
MLA_OUT_WIDTH = MLA_HEADS * MLA_V_DIM
MLA_SCALE = MLA_QK_DIM ** -0.5

LANES = 128
MXU_TILE = 256
VMEM_LIMIT = 56 * 1024 * 1024
ROW_TILE = 512
MOD_COL_TILE = 1024
RET_GROUP = 8
ATTN_SUBTILES = 8
ATTN_HEADS_PER_STEP = 2
BF16_SUBLANES = 16
_VT_ROWS = MLA_V_DIM + BF16_SUBLANES
_Q_SCALE = MLA_SCALE * 1.4426950408889634
_NT = (((1,), (1,)), ((), ()))

_O_RQ = 0
_O_RK = _O_RQ + RET_QK_WIDTH
_O_RV = _O_RK + RET_QK_WIDTH
_O_RG = _O_RV + RET_V_WIDTH
_O_DQ = _O_RG + RET_V_WIDTH
_O_DKV = _O_DQ + MLA_Q_RANK
_O_KR = _O_DKV + MLA_KV_RANK


def _cparams(sem, vmem=VMEM_LIMIT):
    return pltpu.CompilerParams(dimension_semantics=sem, vmem_limit_bytes=vmem)


def _const_spec(shape):
    n = len(shape)
    return pl.BlockSpec(shape, lambda *_: (0,) * n, pipeline_mode=pl.Buffered(1))


def _rms(xf):
    return xf * lax.rsqrt(jnp.mean(xf * xf, axis=-1, keepdims=True) + EPS)


def _sigmoid(a):
    return 0.5 * jnp.tanh(0.5 * a) + 0.5


def _silu(a):
    return a * _sigmoid(a)


def _mod_kernel(c_ref, w_ref, b_ref, o_ref):
    s = _silu(c_ref[...])
    o_ref[0] = jnp.dot(s, w_ref[0], preferred_element_type=F32,
                       precision=lax.Precision.HIGHEST) + b_ref[0]


def _mod_call(cc, w_ada, b_ada):
    depth, d, nd = w_ada.shape
    rows = cc.shape[0]
    tn = MOD_COL_TILE
    return pl.pallas_call(
        _mod_kernel,
        grid=(depth, nd // tn),
        in_specs=[
            pl.BlockSpec((rows, d), lambda l, j: (0, 0)),
            pl.BlockSpec((1, d, tn), lambda l, j: (l, 0, j)),
            pl.BlockSpec((1, 1, tn), lambda l, j: (l, 0, j)),
        ],
        out_specs=pl.BlockSpec((1, rows, tn), lambda l, j: (l, 0, j)),
        out_shape=jax.ShapeDtypeStruct((depth, rows, nd), F32),
        compiler_params=_cparams(("arbitrary", "arbitrary")),
        name="mod",
    )(cc, w_ada, b_ada.reshape(depth, 1, nd))


def _ffn_kernel(x_ref, sh_ref, sc_ref, g_ref, w1_ref, w3_ref, w2_ref, *rest, final):
    if final:
        fn_ref, o_ref = rest
    else:
        (o_ref,) = rest
    xf = x_ref[0]
    h = _rms(xf) * (1.0 + sc_ref[0]) + sh_ref[0]
    hb = h.astype(BF16)
    a = jnp.dot(hb, w1_ref[...], preferred_element_type=F32)
    b = jnp.dot(hb, w3_ref[...], preferred_element_type=F32)
    u = (_silu(a) * b).astype(BF16)
    y = jnp.dot(u, w2_ref[...], preferred_element_type=F32)
    xn = xf + (0.5 * g_ref[0]) * y
    if final:
        xn = _rms(xn) * fn_ref[...]
    o_ref[0] = xn


def _mod_spec(d, k):
    return pl.BlockSpec((1, 1, d), lambda b, i: (b, 0, k))


def _ffn_call(x, mod, k0, w1, w3, w2, final_gain=None):
    bsz, seq, d = x.shape
    dff = w1.shape[1]
    tm = min(ROW_TILE, seq)
    final = final_gain is not None
    in_specs = [
        pl.BlockSpec((1, tm, d), lambda b, i: (b, i, 0)),
        _mod_spec(d, k0), _mod_spec(d, k0 + 1), _mod_spec(d, k0 + 2),
        _const_spec((d, dff)), _const_spec((d, dff)), _const_spec((dff, d)),
    ]
    args = [x, mod, mod, mod, w1, w3, w2]
    if final:
        in_specs.append(_const_spec((1, d)))
        args.append(final_gain.reshape(1, d))
    return pl.pallas_call(
        functools.partial(_ffn_kernel, final=final),
        grid=(bsz, seq // tm),
        in_specs=in_specs,
        out_specs=pl.BlockSpec((1, tm, d), lambda b, i: (b, i, 0)),
        out_shape=jax.ShapeDtypeStruct(x.shape, F32),
        compiler_params=_cparams(("parallel", "parallel")),
        name="ffn_final" if final else "ffn",
    )(*args)


def _rope(y, cos, sin, first_half, off):
    partner = jnp.where(first_half, pltpu.roll(y, LANES - off, 1), pltpu.roll(y, off, 1))
    return y * cos + partner * sin


def _inproj_kernel(x_ref, sh_ref, sc_ref, w_ref, wkr_ref, wg_ref, qn_ref, kvn_ref,
                   wuqt_ref, wuk_ref, wvt_ref, wkt_ref,
                   rcos_ref, rsin_ref, mcos_ref, msin_ref, mtab_ref, rtab_ref,
                   rq_ref, rk_ref, rv_ref, rg_ref, qt_ref, km_ref, vt_ref, gr_ref, gm_ref):
    tm = x_ref.shape[1]
    xf = x_ref[0]
    hb = (_rms(xf) * (1.0 + sc_ref[0]) + sh_ref[0]).astype(BF16)

    def proj(a, b):
        return jnp.dot(hb, w_ref[:, a:b], preferred_element_type=F32)

    lane = lax.broadcasted_iota(jnp.int32, (tm, LANES), 1)
    ret_first = (lane % (RET_QK_DIM // 2)) < (RET_QK_DIM // 4)
    mla_first = (lane % (MLA_ROPE_DIM // 2)) < (MLA_ROPE_DIM // 4)
    rcos, rsin = rcos_ref[...], rsin_ref[...]

    dq_raw = proj(_O_DQ, _O_DKV)
    dkv_raw = proj(_O_DKV, _O_KR)
    kr_raw = jnp.dot(hb, wkr_ref[...], preferred_element_type=F32)
    q = proj(_O_RQ, _O_RK)
    kt = lax.dot_general(wkt_ref[...], hb, _NT, preferred_element_type=F32) * RET_SCALE
    rv_ref[0] = proj(_O_RV, _O_RG).astype(BF16)
    rg_ref[0] = proj(_O_RG, _O_DQ).astype(BF16)
    dq = (_rms(dq_raw) * qn_ref[...]).astype(BF16)
    dkv = (_rms(dkv_raw) * kvn_ref[...]).astype(BF16)
    qt = lax.dot_general(wuqt_ref[...], dq, _NT, preferred_element_type=F32) * _Q_SCALE
    kn = jnp.dot(dkv, wuk_ref[...], preferred_element_type=F32)
    vt = lax.dot_general(wvt_ref[...], dkv, _NT, preferred_element_type=F32)
    gr_ref[0] = jnp.dot(hb, wg_ref[:, :RET_V_WIDTH], preferred_element_type=F32).astype(BF16)
    gm_ref[0] = jnp.dot(hb, wg_ref[:, RET_V_WIDTH:], preferred_element_type=F32).astype(BF16)

    for g in range(RET_QK_WIDTH // LANES):
        sl = slice(g * LANES, (g + 1) * LANES)
        rq_ref[0, :, sl] = _rope(q[:, sl], rcos, rsin, ret_first, RET_QK_DIM // 4).astype(BF16)

    nr = RET_QK_DIM // 4
    rcos_r, rsin_r = rtab_ref[0:nr, :], rtab_ref[nr:2 * nr, :]
    rcos_c, rsin_c = rtab_ref[2 * nr:3 * nr, :], rtab_ref[3 * nr:4 * nr, :]
    for h in range(RET_HEADS):
        r0 = h * RET_QK_DIM
        a, b = kt[r0:r0 + nr], kt[r0 + nr:r0 + 2 * nr]
        c, d = kt[r0 + 2 * nr:r0 + 3 * nr], kt[r0 + 3 * nr:r0 + 4 * nr]
        head = jnp.concatenate([a * rcos_r - b * rsin_r, b * rcos_r + a * rsin_r,
                                c * rcos_c - d * rsin_c, d * rcos_c + c * rsin_c], axis=0)
        rk_ref[0, r0:r0 + RET_QK_DIM, :] = head.astype(BF16)

    nf = MLA_ROPE_DIM // 4
    cos_r, sin_r = mtab_ref[0:nf, :], mtab_ref[nf:2 * nf, :]
    cos_c, sin_c = mtab_ref[2 * nf:3 * nf, :], mtab_ref[3 * nf:4 * nf, :]
    for h in range(MLA_HEADS):
        r0 = h * LANES + MLA_NOPE_DIM
        a, b = qt[r0:r0 + nf], qt[r0 + nf:r0 + 2 * nf]
        c, d = qt[r0 + 2 * nf:r0 + 3 * nf], qt[r0 + 3 * nf:r0 + 4 * nf]
        head = jnp.concatenate([
            qt[h * LANES:r0],
            a * cos_r - b * sin_r, b * cos_r + a * sin_r,
            c * cos_c - d * sin_c, d * cos_c + c * sin_c,
            qt[r0 + MLA_ROPE_DIM:(h + 1) * LANES]], axis=0)
        qt_ref[0, h * LANES:(h + 1) * LANES, :] = head.astype(BF16)

    kr = _rope(kr_raw, mcos_ref[...], msin_ref[...], mla_first, MLA_ROPE_DIM // 4)
    for h in range(MLA_HEADS):
        sl = slice(h * LANES, (h + 1) * LANES)
        km_ref[0, :, sl] = (kn[:, sl] + kr).astype(BF16)
    pad = _VT_ROWS - MLA_V_DIM
    ones_row = (lax.broadcasted_iota(jnp.int32, (pad, tm), 0) == 0).astype(BF16)
    for h in range(MLA_HEADS):
        vt_ref[0, h * _VT_ROWS:h * _VT_ROWS + MLA_V_DIM, :] = vt[h * MLA_V_DIM:(h + 1) * MLA_V_DIM].astype(BF16)
        vt_ref[0, h * _VT_ROWS + MLA_V_DIM:(h + 1) * _VT_ROWS, :] = ones_row


def _inproj_call(x, mod, k0, w, wkr, wg, q_norm, kv_norm, wuqt, wuk, wvt, wkt, tables):
    bsz, seq, d = x.shape
    tm = min(ROW_TILE, seq)
    row = lambda w_: pl.BlockSpec((1, tm, w_), lambda b, i: (b, i, 0))
    col = lambda r_: pl.BlockSpec((1, r_, tm), lambda b, i: (b, 0, i))
    tab = pl.BlockSpec((tm, LANES), lambda b, i: (i, 0))
    ttab = lambda r_: pl.BlockSpec((r_, tm), lambda b, i: (0, i))
    n_q, n_vt = MLA_HEADS * LANES, MLA_HEADS * _VT_ROWS
    shapes = [(seq, RET_QK_WIDTH), (RET_QK_WIDTH, seq), (seq, RET_V_WIDTH), (seq, RET_V_WIDTH),
              (n_q, seq), (seq, n_q), (n_vt, seq), (seq, RET_V_WIDTH), (seq, RET_V_WIDTH)]
    dtypes = (BF16,) * len(shapes)
    out_specs = [col(s[0]) if i in (1, 4, 6) else row(s[1]) for i, s in enumerate(shapes)]
    return pl.pallas_call(
        _inproj_kernel,
        grid=(bsz, seq // tm),
        in_specs=[
            row(d), _mod_spec(d, k0), _mod_spec(d, k0 + 1),
            _const_spec(w.shape), _const_spec(wkr.shape), _const_spec(wg.shape),
            _const_spec((1, MLA_Q_RANK)), _const_spec((1, MLA_KV_RANK)),
            _const_spec(wuqt.shape), _const_spec(wuk.shape), _const_spec(wvt.shape), _const_spec(wkt.shape),
            tab, tab, tab, tab, ttab(MLA_ROPE_DIM), ttab(RET_QK_DIM),
        ],
        out_specs=out_specs,
        out_shape=[jax.ShapeDtypeStruct((bsz,) + s, dt) for s, dt in zip(shapes, dtypes)],
        compiler_params=_cparams(("parallel", "parallel")),
        name="inproj",
    )(x, mod, mod, w, wkr, wg, q_norm.reshape(1, -1), kv_norm.reshape(1, -1), wuqt, wuk, wvt, wkt, *tables)


def _ret_state_kernel(kf_ref, vf_ref, kb_ref, vb_ref, decf_ref, decb_ref, sf0_ref, sb0_ref,
                      sfs_ref, sbs_ref, sf_ref, sb_ref, dm_ref, qs_ref,
                      st_f, st_b, ks_f, ks_b, cd_f, cd_b, *, group):
    b, s = pl.program_id(0), pl.program_id(1)
    c, dk, dv = RET_CHUNK, RET_QK_DIM, RET_V_DIM

    @pl.when(s == 0)
    def _():
        st_f[...] = sf0_ref[0]
        st_b[...] = sb0_ref[0]
        j = lax.broadcasted_iota(jnp.int32, (1, c), 1).astype(F32)
        for h in range(RET_HEADS):
            lgf = -jnp.exp(decf_ref[h])
            lgb = -jnp.exp(decb_ref[h])
            ks_f[h] = jnp.exp(lgf * (c - 1.0 - j))
            ks_b[h] = jnp.exp(lgb * j)
            cd_f[h] = jnp.broadcast_to(jnp.exp(lgf * c), (1, dv))
            cd_b[h] = jnp.broadcast_to(jnp.exp(lgb * c), (1, dv))

    @pl.when((b == 0) & (s == 0))
    def _():
        ii = lax.broadcasted_iota(jnp.int32, (c, c), 0).astype(F32)
        jj = lax.broadcasted_iota(jnp.int32, (c, c), 1).astype(F32)
        diff = ii - jj
        row = lax.broadcasted_iota(jnp.int32, (c, 2 * dk), 0).astype(F32)
        fwd_half = lax.broadcasted_iota(jnp.int32, (c, 2 * dk), 1) < dk
        for h in range(RET_HEADS):
            lgf = -jnp.exp(decf_ref[h])
            lgb = -jnp.exp(decb_ref[h])
            dm_ref[h] = (jnp.where(diff >= 0, jnp.exp(lgf * jnp.maximum(diff, 0.0)), 0.0)
                         + jnp.where(diff <= 0, jnp.exp(lgb * jnp.maximum(-diff, 0.0)), 0.0))
            qs_ref[h] = jnp.exp(jnp.where(fwd_half, lgf * (row + 1.0), lgb * (c - row)))

    kv = {}
    for g in range(group):
        gb = group - 1 - g
        for h in range(RET_HEADS):
            kf = (kf_ref[0, h * dk:(h + 1) * dk, g * c:(g + 1) * c].astype(F32) * ks_f[h]).astype(BF16)
            kv["f", g, h] = jnp.dot(kf, vf_ref[0, g * c:(g + 1) * c, h * dv:(h + 1) * dv],
                                    preferred_element_type=F32)
            kb = (kb_ref[0, h * dk:(h + 1) * dk, gb * c:(gb + 1) * c].astype(F32) * ks_b[h]).astype(BF16)
            kv["b", g, h] = jnp.dot(kb, vb_ref[0, gb * c:(gb + 1) * c, h * dv:(h + 1) * dv],
                                    preferred_element_type=F32)
    for h in range(RET_HEADS):
        sf, sb = st_f[h], st_b[h]
        for g in range(group):
            gb = group - 1 - g
            sfs_ref[0, g, h * dk:(h + 1) * dk, :] = sf.astype(BF16)
            sf = cd_f[h] * sf + kv["f", g, h]
            sbs_ref[0, gb, h * dk:(h + 1) * dk, :] = sb.astype(BF16)
            sb = cd_b[h] * sb + kv["b", g, h]
        st_f[h], st_b[h] = sf, sb

    @pl.when(s == pl.num_programs(1) - 1)
    def _():
        sf_ref[0] = st_f[...]
        sb_ref[0] = st_b[...]


def _ret_state_call(rkt, rv, dec_f, dec_b, s0_f, s0_b):
    bsz, seq, _ = rv.shape
    c = RET_CHUNK
    nc = seq // c
    group = min(RET_GROUP, nc)
    ns = nc // group
    h, dk, dv = RET_HEADS, RET_QK_DIM, RET_V_DIM
    st_spec = pl.BlockSpec((1, h, dk, dv), lambda b, s: (b, 0, 0, 0))
    dec_spec = pl.BlockSpec((h, 1, 1), lambda b, s: (0, 0, 0))
    tab = lambda w_: pl.BlockSpec((h, c, w_), lambda b, s: (0, 0, 0))
    return pl.pallas_call(
        functools.partial(_ret_state_kernel, group=group),
        grid=(bsz, ns),
        in_specs=[
            pl.BlockSpec((1, h * dk, group * c), lambda b, s: (b, 0, s)),
            pl.BlockSpec((1, group * c, h * dv), lambda b, s: (b, s, 0)),
            pl.BlockSpec((1, h * dk, group * c), lambda b, s: (b, 0, ns - 1 - s)),
            pl.BlockSpec((1, group * c, h * dv), lambda b, s: (b, ns - 1 - s, 0)),
            dec_spec, dec_spec, st_spec, st_spec],
        out_specs=[
            pl.BlockSpec((1, group, h * dk, dv), lambda b, s: (b, s, 0, 0)),
            pl.BlockSpec((1, group, h * dk, dv), lambda b, s: (b, ns - 1 - s, 0, 0)),
            st_spec, st_spec, tab(c), tab(2 * dk)],
        out_shape=[jax.ShapeDtypeStruct((bsz, nc, h * dk, dv), BF16)] * 2
        + [jax.ShapeDtypeStruct((bsz, h, dk, dv), F32)] * 2
        + [jax.ShapeDtypeStruct((h, c, c), F32), jax.ShapeDtypeStruct((h, c, 2 * dk), F32)],
        scratch_shapes=[
            pltpu.VMEM((h, dk, dv), F32), pltpu.VMEM((h, dk, dv), F32),
            pltpu.VMEM((h, 1, c), F32), pltpu.VMEM((h, 1, c), F32),
            pltpu.VMEM((h, 1, dv), F32), pltpu.VMEM((h, 1, dv), F32),
        ],
        compiler_params=_cparams(("arbitrary", "arbitrary")),
        name="ret_state",
    )(rkt, rv, rkt, rv, dec_f.reshape(h, 1, 1), dec_b.reshape(h, 1, 1), s0_f, s0_b)


def _attn_kernel(qt_ref, *refs, tq, tk, depth):
    o_ref = refs[-1]
    sources = [(refs[i], refs[i + 1]) for i in range(0, len(refs) - 1, 2)]
    chunks = [(k_ref, vt_ref, j) for k_ref, vt_ref in sources for j in range(k_ref.shape[1] // tk)]
    nk = len(chunks)
    heads = range(qt_ref.shape[1] // LANES)
    rounds = [(qi, n) for n in range(nk) for qi in range(qt_ref.shape[2] // tq)]

    def scores(hd, qi, n):
        k_ref, _, j = chunks[n]
        k = k_ref[0, j * tk:(j + 1) * tk, hd * LANES:(hd + 1) * LANES]
        qt = qt_ref[0, hd * LANES:(hd + 1) * LANES, qi * tq:(qi + 1) * tq]
        return jnp.dot(k, qt, preferred_element_type=F32)

    pending, m_all, acc_all = {}, {}, {}
    for r in range(-depth, len(rounds)):
        if r + depth < len(rounds):
            for hd in heads:
                pending[hd, r + depth] = scores(hd, *rounds[r + depth])
        if r < 0:
            continue
        qi, n = rounds[r]
        if n == 0:
            m_all[qi] = [jnp.full((1, tq), -jnp.inf, F32) for _ in heads]
            acc_all[qi] = [jnp.zeros((_VT_ROWS, tq), F32) for _ in heads]
        m, acc = m_all[qi], acc_all[qi]
        _, vt_ref, j = chunks[n]
        pts, alphas = [], []
        for hd in heads:
            st = pending.pop((hd, r))
            m_new = jnp.maximum(m[hd], jnp.max(st, axis=0, keepdims=True))
            alphas.append(jnp.exp2(m[hd] - m_new))
            pts.append(jnp.exp2(st - m_new).astype(BF16))
            m[hd] = m_new
        for hd in heads:
            vt = vt_ref[0, hd * _VT_ROWS:(hd + 1) * _VT_ROWS, j * tk:(j + 1) * tk]
            acc[hd] = alphas[hd] * acc[hd] + jnp.dot(vt, pts[hd], preferred_element_type=F32)
        if n == nk - 1:
            outs = [a[:MLA_V_DIM] / a[MLA_V_DIM:MLA_V_DIM + 1] for a in acc]
            o_ref[0, qi * tq:(qi + 1) * tq, :] = jnp.concatenate(outs, axis=0).T.astype(o_ref.dtype)


def _attn_call(qt, *kv):
    bsz, _, seq = qt.shape
    tq = tk = MXU_TILE
    tqs = min(ATTN_SUBTILES * tq, seq)
    hps = ATTN_HEADS_PER_STEP
    in_specs = [pl.BlockSpec((1, hps * LANES, tqs), lambda b, h, i: (b, h, i))]
    for km, vt in zip(kv[::2], kv[1::2]):
        lk = km.shape[1]
        in_specs += [pl.BlockSpec((1, lk, hps * LANES), lambda b, h, i: (b, 0, h)),
                     pl.BlockSpec((1, hps * _VT_ROWS, lk), lambda b, h, i: (b, h, 0))]
    return pl.pallas_call(
        functools.partial(_attn_kernel, tq=tq, tk=tk, depth=2),
        grid=(bsz, MLA_HEADS // hps, seq // tqs),
        in_specs=in_specs,
        out_specs=pl.BlockSpec((1, tqs, hps * MLA_V_DIM), lambda b, h, i: (b, i, h)),
        out_shape=jax.ShapeDtypeStruct((bsz, seq, MLA_OUT_WIDTH), BF16),
        compiler_params=_cparams(("parallel", "parallel", "parallel")),
        name="attention",
    )(qt, *kv)


def _merge_kernel(x_ref, rq_ref, rkt_ref, rv_ref, sfs_ref, sbs_ref, dm_ref, qs_ref,
                  rg_ref, at_ref, gr_ref, gm_ref, gn_ref, g2_ref,
                  wr_ref, wm_ref, wo_ref, o_ref, a_scr):
    c, dk, dv = RET_CHUNK, RET_QK_DIM, RET_V_DIM
    units = [(ci, h) for ci in range(x_ref.shape[1] // c) for h in range(RET_HEADS)]
    lane = lax.broadcasted_iota(jnp.int32, (c, 2 * dk), 1)
    low = lane < dk

    def q_pair(ci, h):
        g = h // 2
        return rq_ref[0, ci * c:(ci + 1) * c, g * 2 * dk:(g + 1) * 2 * dk]

    scores = {}
    for ci, h in units:
        mine = low if h % 2 == 0 else jnp.logical_not(low)
        qm = jnp.where(mine, q_pair(ci, h), jnp.zeros((), BF16))
        g = h // 2
        kt = rkt_ref[0, g * 2 * dk:(g + 1) * 2 * dk, ci * c:(ci + 1) * c]
        scores[ci, h] = jnp.dot(qm, kt, preferred_element_type=F32)
    outs = {}
    for ci, h in units:
        p = (scores.pop((ci, h)) * dm_ref[h]).astype(BF16)
        qf = q_pair(ci, h).astype(F32)
        qr = pltpu.roll(qf, dk, 1)
        both = jnp.where(low, qf, qr) if h % 2 == 0 else jnp.where(low, qr, qf)
        lhs = jnp.concatenate([p, (both * qs_ref[h]).astype(BF16)], axis=1)
        rhs = jnp.concatenate([rv_ref[0, ci * c:(ci + 1) * c, h * dv:(h + 1) * dv],
                               sfs_ref[0, ci, h * dk:(h + 1) * dk, :],
                               sbs_ref[0, ci, h * dk:(h + 1) * dk, :]], axis=0)
        outs[ci, h] = jnp.dot(lhs, rhs, preferred_element_type=F32)
    mla_l = jnp.dot(at_ref[0], wm_ref[...], preferred_element_type=F32)
    for ci, h in units:
        rows, sl = slice(ci * c, (ci + 1) * c), slice(h * dv, (h + 1) * dv)
        o = outs.pop((ci, h))
        d = o - jnp.mean(o, axis=-1, keepdims=True)
        nrm = d * lax.rsqrt(jnp.mean(d * d, axis=-1, keepdims=True) + EPS) * gn_ref[:, sl]
        a_scr[rows, sl] = (_silu(rg_ref[0, rows, sl].astype(F32)) * nrm).astype(BF16)
    ret_l = jnp.dot(a_scr[...], wr_ref[...], preferred_element_type=F32)
    mrg = _sigmoid(gr_ref[0].astype(F32)) * ret_l + _sigmoid(gm_ref[0].astype(F32)) * mla_l
    out = jnp.dot(mrg.astype(BF16), wo_ref[...], preferred_element_type=F32)
    o_ref[0] = x_ref[0] + g2_ref[0] * out


def _merge_call(x, rq, rkt, rv, sfs, sbs, dm, qs, rg, att, gr, gm, gn, mod, k_gate, wr, wm, wo):
    bsz, seq, d = x.shape
    tm = min(ROW_TILE, seq)
    nch = tm // RET_CHUNK
    row = lambda w_: pl.BlockSpec((1, tm, w_), lambda b, i: (b, i, 0))
    state = pl.BlockSpec((1, nch, RET_QK_WIDTH, RET_V_DIM), lambda b, i: (b, i, 0, 0))
    return pl.pallas_call(
        _merge_kernel,
        grid=(bsz, seq // tm),
        in_specs=[row(d), row(RET_QK_WIDTH), pl.BlockSpec((1, RET_QK_WIDTH, tm), lambda b, i: (b, 0, i)),
                  row(RET_V_WIDTH), state, state, _const_spec(dm.shape), _const_spec(qs.shape),
                  row(RET_V_WIDTH), row(MLA_OUT_WIDTH),
                  row(RET_V_WIDTH), row(RET_V_WIDTH), _const_spec((1, RET_V_WIDTH)), _mod_spec(d, k_gate),
                  _const_spec(wr.shape), _const_spec(wm.shape), _const_spec(wo.shape)],
        out_specs=row(d),
        out_shape=jax.ShapeDtypeStruct(x.shape, F32),
        scratch_shapes=[pltpu.VMEM((tm, RET_V_WIDTH), BF16)],
        compiler_params=_cparams(("parallel", "parallel")),
        name="merge",
    )(x, rq, rkt, rv, sfs, sbs, dm, qs, rg, att, gr, gm, gn.reshape(1, -1), mod, wr, wm, wo)


def _rope_tables(length, n_freq):
    f32 = np.float32
    rows = length // GRID_W
    row = np.repeat(np.arange(rows, dtype=f32), GRID_W)
    col = np.tile(np.arange(GRID_W, dtype=f32), rows)
    inv_freq = np.power(f32(ROPE_BASE), -np.arange(n_freq, dtype=f32) / f32(n_freq)).astype(f32)
    ang_r = row[:, None] * inv_freq[None, :]
    ang_c = col[:, None] * inv_freq[None, :]
    return np.cos(ang_r), np.sin(ang_r), np.cos(ang_c), np.sin(ang_c)


def _pair_tables(t):
    cos_r, sin_r, cos_c, sin_c = t
    return (np.concatenate([cos_r, cos_r, cos_c, cos_c], axis=-1),
            np.concatenate([-sin_r, sin_r, -sin_c, sin_c], axis=-1))


def _tables(length, rope):
    f32 = np.float32
    nr, nm = RET_QK_DIM // 4, MLA_ROPE_DIM // 4
    if rope:
        rt, mt = _rope_tables(length, nr), _rope_tables(length, nm)
    else:
        one, zero = np.ones((length, 1), f32), np.zeros((length, 1), f32)
        rt = tuple(np.tile(v, (1, nr)) for v in (one, zero, one, zero))
        mt = tuple(np.tile(v, (1, nm)) for v in (one, zero, one, zero))
    rc, rs = _pair_tables(rt)
    mc, ms = _pair_tables(mt)
    reps = LANES // RET_QK_DIM
    tail = LANES - MLA_NOPE_DIM - MLA_ROPE_DIM
    mcos = np.concatenate([np.ones((length, MLA_NOPE_DIM), f32), mc, np.ones((length, tail), f32)], axis=-1)
    msin = np.concatenate([np.zeros((length, MLA_NOPE_DIM), f32), ms, np.zeros((length, tail), f32)], axis=-1)
    mtab = np.ascontiguousarray(np.concatenate(mt, axis=-1).T)
    rtab = np.ascontiguousarray(np.concatenate(rt, axis=-1).T)
    return np.tile(rc, (1, reps)), np.tile(rs, (1, reps)), mcos, msin, mtab, rtab


def _prep_w_in(w_in, l):
    o_kr = _O_KR
    lead = w_in[l, :, :o_kr].astype(BF16)
    kr = jnp.pad(w_in[l, :, o_kr:o_kr + MLA_ROPE_DIM].astype(BF16),
                 ((0, 0), (MLA_NOPE_DIM, LANES - MLA_NOPE_DIM - MLA_ROPE_DIM)))
    gates = w_in[l, :, o_kr + MLA_ROPE_DIM:].astype(BF16)
    wkt = w_in[l, :, _O_RK:_O_RV].astype(BF16).T
    return lead, kr, gates, wkt


def _prep_w_uq(w_uq):
    r = w_uq.shape[0]
    w = w_uq.reshape(r, MLA_HEADS, MLA_QK_DIM)
    w = jnp.pad(w, ((0, 0), (0, 0), (0, LANES - MLA_QK_DIM)))
    return w.reshape(r, MLA_HEADS * LANES).T.astype(BF16)


def _prep_w_ukv(w_ukv):
    r = w_ukv.shape[0]
    w = w_ukv.reshape(r, MLA_HEADS, MLA_NOPE_DIM + MLA_V_DIM)
    wk = jnp.pad(w[..., :MLA_NOPE_DIM], ((0, 0), (0, 0), (0, LANES - MLA_NOPE_DIM)))
    wv = w[..., MLA_NOPE_DIM:]
    return wk.reshape(r, MLA_HEADS * LANES).astype(BF16), wv.reshape(r, MLA_OUT_WIDTH).T.astype(BF16)


def kernel(x, c, ctx, c_ctx, w_ada, b_ada, ffn1_w1, ffn1_w3, ffn1_w2, ffn2_w1, ffn2_w3, ffn2_w2, w_in,
           ret_decay_fwd, ret_decay_bwd, ret_gn, mla_q_norm, mla_kv_norm, w_uq, w_ukv, w_ret_out,
           w_mla_out, w_o, final_norm):
    bsz, seq, d = x.shape
    lc = ctx.shape[1]
    depth = w_ada.shape[0]

    rows = -(-(bsz + 1) // 8) * 8
    cc = jnp.zeros((rows, d), F32).at[:bsz].set(c).at[bsz].set(c_ctx)
    mod_all = _mod_call(cc, w_ada, b_ada)

    tab_lat = _tables(seq, True)
    tab_ctx = _tables(lc, False)
    zero_state = jnp.zeros((bsz, RET_HEADS, RET_QK_DIM, RET_V_DIM), F32)

    xc = ctx
    for l in range(depth):
        last = l == depth - 1
        mod = mod_all[l, :bsz][:, None, :]
        mod_c = jnp.broadcast_to(mod_all[l, bsz][None, None, :], (bsz, 1, N_MOD * d))
        f1 = (ffn1_w1[l].astype(BF16), ffn1_w3[l].astype(BF16), ffn1_w2[l].astype(BF16))
        f2 = (ffn2_w1[l].astype(BF16), ffn2_w3[l].astype(BF16), ffn2_w2[l].astype(BF16))
        w_lead, w_kr, w_gates, wkt_l = _prep_w_in(w_in, l)
        wuqt_l = _prep_w_uq(w_uq[l])
        wuk_l, wvt_l = _prep_w_ukv(w_ukv[l])
        wr, wm, wo = w_ret_out[l].astype(BF16), w_mla_out[l].astype(BF16), w_o[l].astype(BF16)

        x = _ffn_call(x, mod, 0, *f1)
        xc = _ffn_call(xc, mod_c, 0, *f1)

        mla_w = (w_lead, w_kr, w_gates, mla_q_norm[l], mla_kv_norm[l], wuqt_l, wuk_l, wvt_l, wkt_l)
        rq, rkt, rv, rg, qt, km, vt, gr, gm = _inproj_call(x, mod, 3, *mla_w, tab_lat)
        crq, crkt, crv, crg, cqt, ckm, cvt, cgr, cgm = _inproj_call(xc, mod_c, 3, *mla_w, tab_ctx)

        dec = (ret_decay_fwd[l], ret_decay_bwd[l])
        csf, csb, sc_f, sc_b, dm, qs = _ret_state_call(crkt, crv, *dec, zero_state, zero_state)
        sfs, sbs, _, _, _, _ = _ret_state_call(rkt, rv, *dec, sc_f, sc_b)

        att = _attn_call(qt, km, vt, ckm, cvt)
        x = _merge_call(x, rq, rkt, rv, sfs, sbs, dm, qs, rg, att, gr, gm, ret_gn[l], mod, 5, wr, wm, wo)
        x = _ffn_call(x, mod, 6, *f2, final_gain=final_norm if last else None)
        if not last:
            att_c = _attn_call(cqt, ckm, cvt)
            xc = _merge_call(xc, crq, crkt, crv, csf, csb, dm, qs, crg, att_c, cgr, cgm, ret_gn[l],
                             mod_c, 5, wr, wm, wo)
            xc = _ffn_call(xc, mod_c, 6, *f2)
    return x
```

```python
import functools

import jax
import jax.numpy as jnp
import numpy as np
from jax import lax
from jax.experimental import pallas as pl
from jax.experimental.pallas import tpu as pltpu

F32 = jnp.float32
BF16 = jnp.bfloat16

GRID_W = 64
EPS = 1e-6
ROPE_BASE = 10000.0
N_MOD = 9

RET_HEADS = 8
RET_QK_DIM = 64
RET_V_DIM = 128
RET_CHUNK = 128
RET_QK_WIDTH = RET_HEADS * RET_QK_DIM
RET_V_WIDTH = RET_HEADS * RET_V_DIM
RET_SCALE = RET_QK_DIM ** -0.5

MLA_HEADS = 8
MLA_Q_RANK = 384
MLA_KV_RANK = 256
MLA_NOPE_DIM = 64
MLA_ROPE_DIM = 32
MLA_V_DIM = 64
MLA_QK_DIM = MLA_NOPE_DIM + MLA_ROPE_DIM
MLA_OUT_WIDTH = MLA_HEADS * MLA_V_DIM
MLA_SCALE = MLA_QK_DIM ** -0.5

LANES = 128
MXU_TILE = 256
VMEM_LIMIT = 56 * 1024 * 1024
ROW_TILE = 512
MOD_COL_TILE = 1024
RET_GROUP = 8
ATTN_SUBTILES = 8
ATTN_HEADS_PER_STEP = 2
BF16_SUBLANES = 16
_VT_ROWS = MLA_V_DIM + BF16_SUBLANES
_Q_SCALE = MLA_SCALE * 1.4426950408889634
_NT = (((1,), (1,)), ((), ()))

_O_RQ = 0
_O_RK = _O_RQ + RET_QK_WIDTH
_O_RV = _O_RK + RET_QK_WIDTH
_O_RG = _O_RV + RET_V_WIDTH
_O_DQ = _O_RG + RET_V_WIDTH
_O_DKV = _O_DQ + MLA_Q_RANK
_O_KR = _O_DKV + MLA_KV_RANK


def _cparams(sem, vmem=VMEM_LIMIT):
    return pltpu.CompilerParams(dimension_semantics=sem, vmem_limit_bytes=vmem)


def _const_spec(shape):
    n = len(shape)
    return pl.BlockSpec(shape, lambda *_: (0,) * n, pipeline_mode=pl.Buffered(1))


def _rms(xf):
    return xf * lax.rsqrt(jnp.mean(xf * xf, axis=-1, keepdims=True) + EPS)


def _sigmoid(a):
    return 0.5 * jnp.tanh(0.5 * a) + 0.5


def _silu(a):
    return a * _sigmoid(a)


def _mod_kernel(c_ref, w_ref, b_ref, o_ref):
    s = _silu(c_ref[...])
    o_ref[0] = jnp.dot(s.astype(BF16), w_ref[0].astype(BF16), preferred_element_type=F32) + b_ref[0]


def _mod_call(cc, w_ada, b_ada):
    depth, d, nd = w_ada.shape
    rows = cc.shape[0]
    tn = MOD_COL_TILE
    return pl.pallas_call(
        _mod_kernel,
        grid=(depth, nd // tn),
        in_specs=[
            pl.BlockSpec((rows, d), lambda l, j: (0, 0)),
            pl.BlockSpec((1, d, tn), lambda l, j: (l, 0, j)),
            pl.BlockSpec((1, 1, tn), lambda l, j: (l, 0, j)),
        ],
        out_specs=pl.BlockSpec((1, rows, tn), lambda l, j: (l, 0, j)),
        out_shape=jax.ShapeDtypeStruct((depth, rows, nd), F32),
        compiler_params=_cparams(("arbitrary", "arbitrary")),
        name="mod",
    )(cc, w_ada, b_ada.reshape(depth, 1, nd))


def _ffn_kernel(x_ref, sh_ref, sc_ref, g_ref, w1_ref, w3_ref, w2_ref, *rest, final):
    if final:
        fn_ref, o_ref = rest
    else:
        (o_ref,) = rest
    xf = x_ref[0]
    h = _rms(xf) * (1.0 + sc_ref[0]) + sh_ref[0]
    hb = h.astype(BF16)
    a = jnp.dot(hb, w1_ref[...], preferred_element_type=F32)
    b = jnp.dot(hb, w3_ref[...], preferred_element_type=F32)
    u = (_silu(a) * b).astype(BF16)
    y = jnp.dot(u, w2_ref[...], preferred_element_type=F32)
    xn = xf + (0.5 * g_ref[0]) * y
    if final:
        xn = _rms(xn) * fn_ref[...]
    o_ref[0] = xn


def _mod_spec(d, k):
    return pl.BlockSpec((1, 1, d), lambda b, i: (b, 0, k))


def _ffn_call(x, mod, k0, w1, w3, w2, final_gain=None):
    bsz, seq, d = x.shape
    dff = w1.shape[1]
    tm = min(ROW_TILE, seq)
    final = final_gain is not None
    in_specs = [
        pl.BlockSpec((1, tm, d), lambda b, i: (b, i, 0)),
        _mod_spec(d, k0), _mod_spec(d, k0 + 1), _mod_spec(d, k0 + 2),
        _const_spec((d, dff)), _const_spec((d, dff)), _const_spec((dff, d)),
    ]
    args = [x, mod, mod, mod, w1, w3, w2]
    if final:
        in_specs.append(_const_spec((1, d)))
        args.append(final_gain.reshape(1, d))
    return pl.pallas_call(
        functools.partial(_ffn_kernel, final=final),
        grid=(bsz, seq // tm),
        in_specs=in_specs,
        out_specs=pl.BlockSpec((1, tm, d), lambda b, i: (b, i, 0)),
        out_shape=jax.ShapeDtypeStruct(x.shape, F32),
        compiler_params=_cparams(("parallel", "parallel")),
        name="ffn_final" if final else "ffn",
    )(*args)


def _rope(y, cos, sin, first_half, off):
    partner = jnp.where(first_half, pltpu.roll(y, LANES - off, 1), pltpu.roll(y, off, 1))
    return y * cos + partner * sin


def _inproj_kernel(x_ref, sh_ref, sc_ref, w_ref, wkr_ref, wg_ref, qn_ref, kvn_ref,
                   wuqt_ref, wuk_ref, wvt_ref, wkt_ref,
                   rcos_ref, rsin_ref, mcos_ref, msin_ref, mtab_ref, rtab_ref,
                   rq_ref, rk_ref, rv_ref, rg_ref, qt_ref, km_ref, vt_ref, gr_ref, gm_ref):
    tm = x_ref.shape[1]
    xf = x_ref[0]
    hb = (_rms(xf) * (1.0 + sc_ref[0]) + sh_ref[0]).astype(BF16)

    def proj(a, b):
        return jnp.dot(hb, w_ref[:, a:b], preferred_element_type=F32)

    lane = lax.broadcasted_iota(jnp.int32, (tm, LANES), 1)
    ret_first = (lane % (RET_QK_DIM // 2)) < (RET_QK_DIM // 4)
    mla_first = (lane % (MLA_ROPE_DIM // 2)) < (MLA_ROPE_DIM // 4)
    rcos, rsin = rcos_ref[...], rsin_ref[...]

    dq_raw = proj(_O_DQ, _O_DKV)
    dkv_raw = proj(_O_DKV, _O_KR)
    kr_raw = jnp.dot(hb, wkr_ref[...], preferred_element_type=F32)
    q = proj(_O_RQ, _O_RK)
    kt = lax.dot_general(wkt_ref[...], hb, _NT, preferred_element_type=F32) * RET_SCALE
    rv_ref[0] = proj(_O_RV, _O_RG).astype(BF16)
    rg_ref[0] = proj(_O_RG, _O_DQ).astype(BF16)
    dq = (_rms(dq_raw) * qn_ref[...]).astype(BF16)
    dkv = (_rms(dkv_raw) * kvn_ref[...]).astype(BF16)
    qt = lax.dot_general(wuqt_ref[...], dq, _NT, preferred_element_type=F32) * _Q_SCALE
    kn = jnp.dot(dkv, wuk_ref[...], preferred_element_type=F32)
    vt = lax.dot_general(wvt_ref[...], dkv, _NT, preferred_element_type=F32)
    gr_ref[0] = jnp.dot(hb, wg_ref[:, :RET_V_WIDTH], preferred_element_type=F32).astype(BF16)
    gm_ref[0] = jnp.dot(hb, wg_ref[:, RET_V_WIDTH:], preferred_element_type=F32).astype(BF16)

    for g in range(RET_QK_WIDTH // LANES):
        sl = slice(g * LANES, (g + 1) * LANES)
        rq_ref[0, :, sl] = _rope(q[:, sl], rcos, rsin, ret_first, RET_QK_DIM // 4).astype(BF16)

    nr = RET_QK_DIM // 4
    rcos_r, rsin_r = rtab_ref[0:nr, :], rtab_ref[nr:2 * nr, :]
    rcos_c, rsin_c = rtab_ref[2 * nr:3 * nr, :], rtab_ref[3 * nr:4 * nr, :]
    for h in range(RET_HEADS):
        r0 = h * RET_QK_DIM
        a, b = kt[r0:r0 + nr], kt[r0 + nr:r0 + 2 * nr]
        c, d = kt[r0 + 2 * nr:r0 + 3 * nr], kt[r0 + 3 * nr:r0 + 4 * nr]
        head = jnp.concatenate([a * rcos_r - b * rsin_r, b * rcos_r + a * rsin_r,
                                c * rcos_c - d * rsin_c, d * rcos_c + c * rsin_c], axis=0)
        rk_ref[0, r0:r0 + RET_QK_DIM, :] = head.astype(BF16)

    nf = MLA_ROPE_DIM // 4
    cos_r, sin_r = mtab_ref[0:nf, :], mtab_ref[nf:2 * nf, :]
    cos_c, sin_c = mtab_ref[2 * nf:3 * nf, :], mtab_ref[3 * nf:4 * nf, :]
    for h in range(MLA_HEADS):
        r0 = h * LANES + MLA_NOPE_DIM
        a, b = qt[r0:r0 + nf], qt[r0 + nf:r0 + 2 * nf]
        c, d = qt[r0 + 2 * nf:r0 + 3 * nf], qt[r0 + 3 * nf:r0 + 4 * nf]
        head = jnp.concatenate([
            qt[h * LANES:r0],
            a * cos_r - b * sin_r, b * cos_r + a * sin_r,
            c * cos_c - d * sin_c, d * cos_c + c * sin_c,
            qt[r0 + MLA_ROPE_DIM:(h + 1) * LANES]], axis=0)
        qt_ref[0, h * LANES:(h + 1) * LANES, :] = head.astype(BF16)

    kr = _rope(kr_raw, mcos_ref[...], msin_ref[...], mla_first, MLA_ROPE_DIM // 4)
    for h in range(MLA_HEADS):
        sl = slice(h * LANES, (h + 1) * LANES)
        km_ref[0, :, sl] = (kn[:, sl] + kr).astype(BF16)
    pad = _VT_ROWS - MLA_V_DIM
    ones_row = (lax.broadcasted_iota(jnp.int32, (pad, tm), 0) == 0).astype(BF16)
    for h in range(MLA_HEADS):
        vt_ref[0, h * _VT_ROWS:h * _VT_ROWS + MLA_V_DIM, :] = vt[h * MLA_V_DIM:(h + 1) * MLA_V_DIM].astype(BF16)
        vt_ref[0, h * _VT_ROWS + MLA_V_DIM:(h + 1) * _VT_ROWS, :] = ones_row


def _inproj_call(x, mod, k0, w, wkr, wg, q_norm, kv_norm, wuqt, wuk, wvt, wkt, tables):
    bsz, seq, d = x.shape
    tm = min(ROW_TILE, seq)
    row = lambda w_: pl.BlockSpec((1, tm, w_), lambda b, i: (b, i, 0))
    col = lambda r_: pl.BlockSpec((1, r_, tm), lambda b, i: (b, 0, i))
    tab = pl.BlockSpec((tm, LANES), lambda b, i: (i, 0))
    ttab = lambda r_: pl.BlockSpec((r_, tm), lambda b, i: (0, i))
    n_q, n_vt = MLA_HEADS * LANES, MLA_HEADS * _VT_ROWS
    shapes = [(seq, RET_QK_WIDTH), (RET_QK_WIDTH, seq), (seq, RET_V_WIDTH), (seq, RET_V_WIDTH),
              (n_q, seq), (seq, n_q), (n_vt, seq), (seq, RET_V_WIDTH), (seq, RET_V_WIDTH)]
    dtypes = (BF16,) * len(shapes)
    out_specs = [col(s[0]) if i in (1, 4, 6) else row(s[1]) for i, s in enumerate(shapes)]
    return pl.pallas_call(
        _inproj_kernel,
        grid=(bsz, seq // tm),
        in_specs=[
            row(d), _mod_spec(d, k0), _mod_spec(d, k0 + 1),
            _const_spec(w.shape), _const_spec(wkr.shape), _const_spec(wg.shape),
            _const_spec((1, MLA_Q_RANK)), _const_spec((1, MLA_KV_RANK)),
            _const_spec(wuqt.shape), _const_spec(wuk.shape), _const_spec(wvt.shape), _const_spec(wkt.shape),
            tab, tab, tab, tab, ttab(MLA_ROPE_DIM), ttab(RET_QK_DIM),
        ],
        out_specs=out_specs,
        out_shape=[jax.ShapeDtypeStruct((bsz,) + s, dt) for s, dt in zip(shapes, dtypes)],
        compiler_params=_cparams(("parallel", "parallel")),
        name="inproj",
    )(x, mod, mod, w, wkr, wg, q_norm.reshape(1, -1), kv_norm.reshape(1, -1), wuqt, wuk, wvt, wkt, *tables)


def _ret_state_kernel(kf_ref, vf_ref, kb_ref, vb_ref, decf_ref, decb_ref, sf0_ref, sb0_ref,
                      sfs_ref, sbs_ref, sf_ref, sb_ref, dm_ref, qs_ref,
                      st_f, st_b, ks_f, ks_b, cd_f, cd_b, *, group):
    b, s = pl.program_id(0), pl.program_id(1)
    c, dk, dv = RET_CHUNK, RET_QK_DIM, RET_V_DIM

    @pl.when(s == 0)
    def _():
        st_f[...] = sf0_ref[0]
        st_b[...] = sb0_ref[0]
        j = lax.broadcasted_iota(jnp.int32, (1, c), 1).astype(F32)
        for h in range(RET_HEADS):
            lgf = -jnp.exp(decf_ref[h])
            lgb = -jnp.exp(decb_ref[h])
            ks_f[h] = jnp.exp(lgf * (c - 1.0 - j))
            ks_b[h] = jnp.exp(lgb * j)
            cd_f[h] = jnp.broadcast_to(jnp.exp(lgf * c), (1, dv))
            cd_b[h] = jnp.broadcast_to(jnp.exp(lgb * c), (1, dv))

    @pl.when((b == 0) & (s == 0))
    def _():
        ii = lax.broadcasted_iota(jnp.int32, (c, c), 0).astype(F32)
        jj = lax.broadcasted_iota(jnp.int32, (c, c), 1).astype(F32)
        diff = ii - jj
        row = lax.broadcasted_iota(jnp.int32, (c, 2 * dk), 0).astype(F32)
        fwd_half = lax.broadcasted_iota(jnp.int32, (c, 2 * dk), 1) < dk
        for h in range(RET_HEADS):
            lgf = -jnp.exp(decf_ref[h])
            lgb = -jnp.exp(decb_ref[h])
            dm_ref[h] = (jnp.where(diff >= 0, jnp.exp(lgf * jnp.maximum(diff, 0.0)), 0.0)
                         + jnp.where(diff <= 0, jnp.exp(lgb * jnp.maximum(-diff, 0.0)), 0.0))
            qs_ref[h] = jnp.exp(jnp.where(fwd_half, lgf * (row + 1.0), lgb * (c - row)))

    kv = {}
    for g in range(group):
        gb = group - 1 - g
        for h in range(RET_HEADS):
            kf = (kf_ref[0, h * dk:(h + 1) * dk, g * c:(g + 1) * c].astype(F32) * ks_f[h]).astype(BF16)
            kv["f", g, h] = jnp.dot(kf, vf_ref[0, g * c:(g + 1) * c, h * dv:(h + 1) * dv],
                                    preferred_element_type=F32)
            kb = (kb_ref[0, h * dk:(h + 1) * dk, gb * c:(gb + 1) * c].astype(F32) * ks_b[h]).astype(BF16)
            kv["b", g, h] = jnp.dot(kb, vb_ref[0, gb * c:(gb + 1) * c, h * dv:(h + 1) * dv],
                                    preferred_element_type=F32)
    for h in range(RET_HEADS):
        sf, sb = st_f[h], st_b[h]
        for g in range(group):
            gb = group - 1 - g
            sfs_ref[0, g, h * dk:(h + 1) * dk, :] = sf.astype(BF16)
            sf = cd_f[h] * sf + kv["f", g, h]
            sbs_ref[0, gb, h * dk:(h + 1) * dk, :] = sb.astype(BF16)
            sb = cd_b[h] * sb + kv["b", g, h]
        st_f[h], st_b[h] = sf, sb

    @pl.when(s == pl.num_programs(1) - 1)
    def _():
        sf_ref[0] = st_f[...]
        sb_ref[0] = st_b[...]


def _ret_state_call(rkt, rv, dec_f, dec_b, s0_f, s0_b):
    bsz, seq, _ = rv.shape
    c = RET_CHUNK
    nc = seq // c
    group = min(RET_GROUP, nc)
    ns = nc // group
    h, dk, dv = RET_HEADS, RET_QK_DIM, RET_V_DIM
    st_spec = pl.BlockSpec((1, h, dk, dv), lambda b, s: (b, 0, 0, 0))
    dec_spec = pl.BlockSpec((h, 1, 1), lambda b, s: (0, 0, 0))
    tab = lambda w_: pl.BlockSpec((h, c, w_), lambda b, s: (0, 0, 0))
    return pl.pallas_call(
        functools.partial(_ret_state_kernel, group=group),
        grid=(bsz, ns),
        in_specs=[
            pl.BlockSpec((1, h * dk, group * c), lambda b, s: (b, 0, s)),
            pl.BlockSpec((1, group * c, h * dv), lambda b, s: (b, s, 0)),
            pl.BlockSpec((1, h * dk, group * c), lambda b, s: (b, 0, ns - 1 - s)),
            pl.BlockSpec((1, group * c, h * dv), lambda b, s: (b, ns - 1 - s, 0)),
            dec_spec, dec_spec, st_spec, st_spec],
        out_specs=[
            pl.BlockSpec((1, group, h * dk, dv), lambda b, s: (b, s, 0, 0)),
            pl.BlockSpec((1, group, h * dk, dv), lambda b, s: (b, ns - 1 - s, 0, 0)),
            st_spec, st_spec, tab(c), tab(2 * dk)],
        out_shape=[jax.ShapeDtypeStruct((bsz, nc, h * dk, dv), BF16)] * 2
        + [jax.ShapeDtypeStruct((bsz, h, dk, dv), F32)] * 2
        + [jax.ShapeDtypeStruct((h, c, c), F32), jax.ShapeDtypeStruct((h, c, 2 * dk), F32)],
        scratch_shapes=[
            pltpu.VMEM((h, dk, dv), F32), pltpu.VMEM((h, dk, dv), F32),
            pltpu.VMEM((h, 1, c), F32), pltpu.VMEM((h, 1, c), F32),
            pltpu.VMEM((h, 1, dv), F32), pltpu.VMEM((h, 1, dv), F32),
        ],
        compiler_params=_cparams(("arbitrary", "arbitrary")),
        name="ret_state",
    )(rkt, rv, rkt, rv, dec_f.reshape(h, 1, 1), dec_b.reshape(h, 1, 1), s0_f, s0_b)


def _attn_kernel(qt_ref, *refs, tq, tk, depth):
    o_ref = refs[-1]
    sources = [(refs[i], refs[i + 1]) for i in range(0, len(refs) - 1, 2)]
    chunks = [(k_ref, vt_ref, j) for k_ref, vt_ref in sources for j in range(k_ref.shape[1] // tk)]
    nk = len(chunks)
    heads = range(qt_ref.shape[1] // LANES)
    rounds = [(qi, n) for n in range(nk) for qi in range(qt_ref.shape[2] // tq)]

    def scores(hd, qi, n):
        k_ref, _, j = chunks[n]
        k = k_ref[0, j * tk:(j + 1) * tk, hd * LANES:(hd + 1) * LANES]
        qt = qt_ref[0, hd * LANES:(hd + 1) * LANES, qi * tq:(qi + 1) * tq]
        return jnp.dot(k, qt, preferred_element_type=F32)

    pending, m_all, acc_all = {}, {}, {}
    for r in range(-depth, len(rounds)):
        if r + depth < len(rounds):
            for hd in heads:
                pending[hd, r + depth] = scores(hd, *rounds[r + depth])
        if r < 0:
            continue
        qi, n = rounds[r]
        if n == 0:
            m_all[qi] = [jnp.full((1, tq), -jnp.inf, F32) for _ in heads]
            acc_all[qi] = [jnp.zeros((_VT_ROWS, tq), F32) for _ in heads]
        m, acc = m_all[qi], acc_all[qi]
        _, vt_ref, j = chunks[n]
        pts, alphas = [], []
        for hd in heads:
            st = pending.pop((hd, r))
            m_new = jnp.maximum(m[hd], jnp.max(st, axis=0, keepdims=True))
            alphas.append(jnp.exp2(m[hd] - m_new))
            pts.append(jnp.exp2(st - m_new).astype(BF16))
            m[hd] = m_new
        for hd in heads:
            vt = vt_ref[0, hd * _VT_ROWS:(hd + 1) * _VT_ROWS, j * tk:(j + 1) * tk]
            acc[hd] = alphas[hd] * acc[hd] + jnp.dot(vt, pts[hd], preferred_element_type=F32)
        if n == nk - 1:
            outs = [a[:MLA_V_DIM] / a[MLA_V_DIM:MLA_V_DIM + 1] for a in acc]
            o_ref[0, qi * tq:(qi + 1) * tq, :] = jnp.concatenate(outs, axis=0).T.astype(o_ref.dtype)


def _attn_call(qt, *kv):
    bsz, _, seq = qt.shape
    tq = tk = MXU_TILE
    tqs = min(ATTN_SUBTILES * tq, seq)
    hps = ATTN_HEADS_PER_STEP
    in_specs = [pl.BlockSpec((1, hps * LANES, tqs), lambda b, h, i: (b, h, i))]
    for km, vt in zip(kv[::2], kv[1::2]):
        lk = km.shape[1]
        in_specs += [pl.BlockSpec((1, lk, hps * LANES), lambda b, h, i: (b, 0, h)),
                     pl.BlockSpec((1, hps * _VT_ROWS, lk), lambda b, h, i: (b, h, 0))]
    return pl.pallas_call(
        functools.partial(_attn_kernel, tq=tq, tk=tk, depth=2),
        grid=(bsz, MLA_HEADS // hps, seq // tqs),
        in_specs=in_specs,
        out_specs=pl.BlockSpec((1, tqs, hps * MLA_V_DIM), lambda b, h, i: (b, i, h)),
        out_shape=jax.ShapeDtypeStruct((bsz, seq, MLA_OUT_WIDTH), BF16),
        compiler_params=_cparams(("parallel", "parallel", "parallel")),
        name="attention",
    )(qt, *kv)


def _merge_kernel(x_ref, rq_ref, rkt_ref, rv_ref, sfs_ref, sbs_ref, dm_ref, qs_ref,
                  rg_ref, at_ref, gr_ref, gm_ref, gn_ref, g2_ref,
                  wr_ref, wm_ref, wo_ref, o_ref, a_scr):
    c, dk, dv = RET_CHUNK, RET_QK_DIM, RET_V_DIM
    units = [(ci, h) for ci in range(x_ref.shape[1] // c) for h in range(RET_HEADS)]
    lane = lax.broadcasted_iota(jnp.int32, (c, 2 * dk), 1)
    low = lane < dk

    def q_pair(ci, h):
        g = h // 2
        return rq_ref[0, ci * c:(ci + 1) * c, g * 2 * dk:(g + 1) * 2 * dk]

    scores = {}
    for ci, h in units:
        mine = low if h % 2 == 0 else jnp.logical_not(low)
        qm = jnp.where(mine, q_pair(ci, h), jnp.zeros((), BF16))
        g = h // 2
        kt = rkt_ref[0, g * 2 * dk:(g + 1) * 2 * dk, ci * c:(ci + 1) * c]
        scores[ci, h] = jnp.dot(qm, kt, preferred_element_type=F32)
    outs = {}
    for ci, h in units:
        p = (scores.pop((ci, h)) * dm_ref[h]).astype(BF16)
        qf = q_pair(ci, h).astype(F32)
        qr = pltpu.roll(qf, dk, 1)
        both = jnp.where(low, qf, qr) if h % 2 == 0 else jnp.where(low, qr, qf)
        lhs = jnp.concatenate([p, (both * qs_ref[h]).astype(BF16)], axis=1)
        rhs = jnp.concatenate([rv_ref[0, ci * c:(ci + 1) * c, h * dv:(h + 1) * dv],
                               sfs_ref[0, ci, h * dk:(h + 1) * dk, :],
                               sbs_ref[0, ci, h * dk:(h + 1) * dk, :]], axis=0)
        outs[ci, h] = jnp.dot(lhs, rhs, preferred_element_type=F32)
    mla_l = jnp.dot(at_ref[0], wm_ref[...], preferred_element_type=F32)
    for ci, h in units:
        rows, sl = slice(ci * c, (ci + 1) * c), slice(h * dv, (h + 1) * dv)
        o = outs.pop((ci, h))
        d = o - jnp.mean(o, axis=-1, keepdims=True)
        nrm = d * lax.rsqrt(jnp.mean(d * d, axis=-1, keepdims=True) + EPS) * gn_ref[:, sl]
        a_scr[rows, sl] = (_silu(rg_ref[0, rows, sl].astype(F32)) * nrm).astype(BF16)
    ret_l = jnp.dot(a_scr[...], wr_ref[...], preferred_element_type=F32)
    mrg = _sigmoid(gr_ref[0].astype(F32)) * ret_l + _sigmoid(gm_ref[0].astype(F32)) * mla_l
    out = jnp.dot(mrg.astype(BF16), wo_ref[...], preferred_element_type=F32)
    o_ref[0] = x_ref[0] + g2_ref[0] * out


def _merge_call(x, rq, rkt, rv, sfs, sbs, dm, qs, rg, att, gr, gm, gn, mod, k_gate, wr, wm, wo):
    bsz, seq, d = x.shape
    tm = min(ROW_TILE, seq)
    nch = tm // RET_CHUNK
    row = lambda w_: pl.BlockSpec((1, tm, w_), lambda b, i: (b, i, 0))
    state = pl.BlockSpec((1, nch, RET_QK_WIDTH, RET_V_DIM), lambda b, i: (b, i, 0, 0))
    return pl.pallas_call(
        _merge_kernel,
        grid=(bsz, seq // tm),
        in_specs=[row(d), row(RET_QK_WIDTH), pl.BlockSpec((1, RET_QK_WIDTH, tm), lambda b, i: (b, 0, i)),
                  row(RET_V_WIDTH), state, state, _const_spec(dm.shape), _const_spec(qs.shape),
                  row(RET_V_WIDTH), row(MLA_OUT_WIDTH),
                  row(RET_V_WIDTH), row(RET_V_WIDTH), _const_spec((1, RET_V_WIDTH)), _mod_spec(d, k_gate),
                  _const_spec(wr.shape), _const_spec(wm.shape), _const_spec(wo.shape)],
        out_specs=row(d),
        out_shape=jax.ShapeDtypeStruct(x.shape, F32),
        scratch_shapes=[pltpu.VMEM((tm, RET_V_WIDTH), BF16)],
        compiler_params=_cparams(("parallel", "parallel")),
        name="merge",
    )(x, rq, rkt, rv, sfs, sbs, dm, qs, rg, att, gr, gm, gn.reshape(1, -1), mod, wr, wm, wo)


def _rope_tables(length, n_freq):
    f32 = np.float32
    rows = length // GRID_W
    row = np.repeat(np.arange(rows, dtype=f32), GRID_W)
    col = np.tile(np.arange(GRID_W, dtype=f32), rows)
    inv_freq = np.power(f32(ROPE_BASE), -np.arange(n_freq, dtype=f32) / f32(n_freq)).astype(f32)
    ang_r = row[:, None] * inv_freq[None, :]
    ang_c = col[:, None] * inv_freq[None, :]
    return np.cos(ang_r), np.sin(ang_r), np.cos(ang_c), np.sin(ang_c)


def _pair_tables(t):
    cos_r, sin_r, cos_c, sin_c = t
    return (np.concatenate([cos_r, cos_r, cos_c, cos_c], axis=-1),
            np.concatenate([-sin_r, sin_r, -sin_c, sin_c], axis=-1))


def _tables(length, rope):
    f32 = np.float32
    nr, nm = RET_QK_DIM // 4, MLA_ROPE_DIM // 4
    if rope:
        rt, mt = _rope_tables(length, nr), _rope_tables(length, nm)
    else:
        one, zero = np.ones((length, 1), f32), np.zeros((length, 1), f32)
        rt = tuple(np.tile(v, (1, nr)) for v in (one, zero, one, zero))
        mt = tuple(np.tile(v, (1, nm)) for v in (one, zero, one, zero))
    rc, rs = _pair_tables(rt)
    mc, ms = _pair_tables(mt)
    reps = LANES // RET_QK_DIM
    tail = LANES - MLA_NOPE_DIM - MLA_ROPE_DIM
    mcos = np.concatenate([np.ones((length, MLA_NOPE_DIM), f32), mc, np.ones((length, tail), f32)], axis=-1)
    msin = np.concatenate([np.zeros((length, MLA_NOPE_DIM), f32), ms, np.zeros((length, tail), f32)], axis=-1)
    mtab = np.ascontiguousarray(np.concatenate(mt, axis=-1).T)
    rtab = np.ascontiguousarray(np.concatenate(rt, axis=-1).T)
    return np.tile(rc, (1, reps)), np.tile(rs, (1, reps)), mcos, msin, mtab, rtab


def _prep_w_in(w_in, l):
    o_kr = _O_KR
    lead = w_in[l, :, :o_kr].astype(BF16)
    kr = jnp.pad(w_in[l, :, o_kr:o_kr + MLA_ROPE_DIM].astype(BF16),
                 ((0, 0), (MLA_NOPE_DIM, LANES - MLA_NOPE_DIM - MLA_ROPE_DIM)))
    gates = w_in[l, :, o_kr + MLA_ROPE_DIM:].astype(BF16)
    wkt = w_in[l, :, _O_RK:_O_RV].astype(BF16).T
    return lead, kr, gates, wkt


def _prep_w_uq(w_uq):
    r = w_uq.shape[0]
    w = w_uq.reshape(r, MLA_HEADS, MLA_QK_DIM)
    w = jnp.pad(w, ((0, 0), (0, 0), (0, LANES - MLA_QK_DIM)))
    return w.reshape(r, MLA_HEADS * LANES).T.astype(BF16)


def _prep_w_ukv(w_ukv):
    r = w_ukv.shape[0]
    w = w_ukv.reshape(r, MLA_HEADS, MLA_NOPE_DIM + MLA_V_DIM)
    wk = jnp.pad(w[..., :MLA_NOPE_DIM], ((0, 0), (0, 0), (0, LANES - MLA_NOPE_DIM)))
    wv = w[..., MLA_NOPE_DIM:]
    return wk.reshape(r, MLA_HEADS * LANES).astype(BF16), wv.reshape(r, MLA_OUT_WIDTH).T.astype(BF16)


def kernel(x, c, ctx, c_ctx, w_ada, b_ada, ffn1_w1, ffn1_w3, ffn1_w2, ffn2_w1, ffn2_w3, ffn2_w2, w_in,
           ret_decay_fwd, ret_decay_bwd, ret_gn, mla_q_norm, mla_kv_norm, w_uq, w_ukv, w_ret_out,
           w_mla_out, w_o, final_norm):
    bsz, seq, d = x.shape
    lc = ctx.shape[1]
    depth = w_ada.shape[0]

    rows = -(-(bsz + 1) // 8) * 8
    cc = jnp.zeros((rows, d), F32).at[:bsz].set(c).at[bsz].set(c_ctx)
    mod_all = _mod_call(cc, w_ada, b_ada)

    tab_lat = _tables(seq, True)
    tab_ctx = _tables(lc, False)
    zero_state = jnp.zeros((bsz, RET_HEADS, RET_QK_DIM, RET_V_DIM), F32)

    xc = ctx
    for l in range(depth):
        last = l == depth - 1
        mod = mod_all[l, :bsz][:, None, :]
        mod_c = jnp.broadcast_to(mod_all[l, bsz][None, None, :], (bsz, 1, N_MOD * d))
        f1 = (ffn1_w1[l].astype(BF16), ffn1_w3[l].astype(BF16), ffn1_w2[l].astype(BF16))
        f2 = (ffn2_w1[l].astype(BF16), ffn2_w3[l].astype(BF16), ffn2_w2[l].astype(BF16))
        w_lead, w_kr, w_gates, wkt_l = _prep_w_in(w_in, l)
        wuqt_l = _prep_w_uq(w_uq[l])
        wuk_l, wvt_l = _prep_w_ukv(w_ukv[l])
        wr, wm, wo = w_ret_out[l].astype(BF16), w_mla_out[l].astype(BF16), w_o[l].astype(BF16)

        x = _ffn_call(x, mod, 0, *f1)
        xc = _ffn_call(xc, mod_c, 0, *f1)

        mla_w = (w_lead, w_kr, w_gates, mla_q_norm[l], mla_kv_norm[l], wuqt_l, wuk_l, wvt_l, wkt_l)
        rq, rkt, rv, rg, qt, km, vt, gr, gm = _inproj_call(x, mod, 3, *mla_w, tab_lat)
        crq, crkt, crv, crg, cqt, ckm, cvt, cgr, cgm = _inproj_call(xc, mod_c, 3, *mla_w, tab_ctx)

        dec = (ret_decay_fwd[l], ret_decay_bwd[l])
        csf, csb, sc_f, sc_b, dm, qs = _ret_state_call(crkt, crv, *dec, zero_state, zero_state)
        sfs, sbs, _, _, _, _ = _ret_state_call(rkt, rv, *dec, sc_f, sc_b)

        att = _attn_call(qt, km, vt, ckm, cvt)
        x = _merge_call(x, rq, rkt, rv, sfs, sbs, dm, qs, rg, att, gr, gm, ret_gn[l], mod, 5, wr, wm, wo)
        x = _ffn_call(x, mod, 6, *f2, final_gain=final_norm if last else None)
        if not last:
            att_c = _attn_call(cqt, ckm, cvt)
            xc = _merge_call(xc, crq, crkt, crv, csf, csb, dm, qs, crg, att_c, cgr, cgm, ret_gn[l],
                             mod_c, 5, wr, wm, wo)
            xc = _ffn_call(xc, mod_c, 6, *f2)
    return x
```

```python
import functools

import jax
import jax.numpy as jnp
import numpy as np
from jax import lax
from jax.experimental import pallas as pl
from jax.experimental.pallas import tpu as pltpu

F32 = jnp.float32
BF16 = jnp.bfloat16

GRID_W = 64
EPS = 1e-6
ROPE_BASE = 10000.0
N_MOD = 9

RET_HEADS = 8
RET_QK_DIM = 64
RET_V_DIM = 128
RET_CHUNK = 128
RET_QK_WIDTH = RET_HEADS * RET_QK_DIM
RET_V_WIDTH = RET_HEADS * RET_V_DIM
RET_SCALE = RET_QK_DIM ** -0.5

MLA_HEADS = 8
MLA_Q_RANK = 384
MLA_KV_RANK = 256
MLA_NOPE_DIM = 64
MLA_ROPE_DIM = 32
MLA_V_DIM = 64
MLA_QK_DIM = MLA_NOPE_DIM + MLA_ROPE_DIM
MLA_OUT_WIDTH = MLA_HEADS * MLA_V_DIM
MLA_SCALE = MLA_QK_DIM ** -0.5

LANES = 128
MXU_TILE = 256
VMEM_LIMIT = 56 * 1024 * 1024
ROW_TILE = 512
MOD_COL_TILE = 1024
RET_GROUP = 8
ATTN_SUBTILES = 8
ATTN_HEADS_PER_STEP = 2
BF16_SUBLANES = 16
_VT_ROWS = MLA_V_DIM + BF16_SUBLANES
_Q_SCALE = MLA_SCALE * 1.4426950408889634
_NT = (((1,), (1,)), ((), ()))

_O_RQ = 0
_O_RK = _O_RQ + RET_QK_WIDTH
_O_RV = _O_RK + RET_QK_WIDTH
_O_RG = _O_RV + RET_V_WIDTH
_O_DQ = _O_RG + RET_V_WIDTH
_O_DKV = _O_DQ + MLA_Q_RANK
_O_KR = _O_DKV + MLA_KV_RANK


def _cparams(sem, vmem=VMEM_LIMIT, fuse_inputs=None):
    return pltpu.CompilerParams(dimension_semantics=sem, vmem_limit_bytes=vmem, allow_input_fusion=fuse_inputs)


def _const_spec(shape):
    n = len(shape)
    return pl.BlockSpec(shape, lambda *_: (0,) * n, pipeline_mode=pl.Buffered(1))


def _rms(xf):
    return xf * lax.rsqrt(jnp.mean(xf * xf, axis=-1, keepdims=True) + EPS)


def _sigmoid(a):
    return 0.5 * jnp.tanh(0.5 * a) + 0.5


def _silu(a):
    return a * _sigmoid(a)


def _mod_kernel(c_ref, w_ref, b_ref, o_ref):
    s = _silu(c_ref[...])
    o_ref[0] = jnp.dot(s.astype(BF16), w_ref[0].astype(BF16), preferred_element_type=F32) + b_ref[0]


def _mod_call(cc, w_ada, b_ada):
    depth, d, nd = w_ada.shape
    rows = cc.shape[0]
    tn = MOD_COL_TILE
    return pl.pallas_call(
        _mod_kernel,
        grid=(depth, nd // tn),
        in_specs=[
            pl.BlockSpec((rows, d), lambda l, j: (0, 0)),
            pl.BlockSpec((1, d, tn), lambda l, j: (l, 0, j)),
            pl.BlockSpec((1, 1, tn), lambda l, j: (l, 0, j)),
        ],
        out_specs=pl.BlockSpec((1, rows, tn), lambda l, j: (l, 0, j)),
        out_shape=jax.ShapeDtypeStruct((depth, rows, nd), F32),
        compiler_params=_cparams(("arbitrary", "arbitrary")),
        name="mod",
    )(cc, w_ada, b_ada.reshape(depth, 1, nd))


def _ffn_kernel(x_ref, sh_ref, sc_ref, g_ref, w1_ref, w3_ref, w2_ref, *rest, final):
    if final:
        fn_ref, o_ref = rest
    else:
        (o_ref,) = rest
    xf = x_ref[0]
    h = _rms(xf) * (1.0 + sc_ref[0]) + sh_ref[0]
    hb = h.astype(BF16)
    a = jnp.dot(hb, w1_ref[...], preferred_element_type=F32)
    b = jnp.dot(hb, w3_ref[...], preferred_element_type=F32)
    u = (_silu(a) * b).astype(BF16)
    y = jnp.dot(u, w2_ref[...], preferred_element_type=F32)
    xn = xf + (0.5 * g_ref[0]) * y
    if final:
        xn = _rms(xn) * fn_ref[...]
    o_ref[0] = xn


def _mod_spec(d, k):
    return pl.BlockSpec((1, 1, d), lambda b, i: (b, 0, k))


def _ffn_call(x, mod, k0, w1, w3, w2, final_gain=None):
    bsz, seq, d = x.shape
    dff = w1.shape[1]
    tm = min(ROW_TILE, seq)
    final = final_gain is not None
    in_specs = [
        pl.BlockSpec((1, tm, d), lambda b, i: (b, i, 0)),
        _mod_spec(d, k0), _mod_spec(d, k0 + 1), _mod_spec(d, k0 + 2),
        _const_spec((d, dff)), _const_spec((d, dff)), _const_spec((dff, d)),
    ]
    args = [x, mod, mod, mod, w1, w3, w2]
    if final:
        in_specs.append(_const_spec((1, d)))
        args.append(final_gain.reshape(1, d))
    return pl.pallas_call(
        functools.partial(_ffn_kernel, final=final),
        grid=(bsz, seq // tm),
        in_specs=in_specs,
        out_specs=pl.BlockSpec((1, tm, d), lambda b, i: (b, i, 0)),
        out_shape=jax.ShapeDtypeStruct(x.shape, F32),
        compiler_params=_cparams(("parallel", "parallel"),
                                 fuse_inputs=[i in (4, 5, 6) for i in range(len(args))]),
        name="ffn_final" if final else "ffn",
    )(*args)


def _rope(y, cos, sin, first_half, off):
    partner = jnp.where(first_half, pltpu.roll(y, LANES - off, 1), pltpu.roll(y, off, 1))
    return y * cos + partner * sin


def _inproj_kernel(x_ref, sh_ref, sc_ref, w_ref, wkr_ref, wg_ref, qn_ref, kvn_ref,
                   wuqt_ref, wuk_ref, wvt_ref, wkt_ref,
                   rcos_ref, rsin_ref, mcos_ref, msin_ref, mtab_ref, rtab_ref,
                   rq_ref, rk_ref, rv_ref, rg_ref, qt_ref, km_ref, vt_ref, gr_ref, gm_ref):
    tm = x_ref.shape[1]
    xf = x_ref[0]
    hb = (_rms(xf) * (1.0 + sc_ref[0]) + sh_ref[0]).astype(BF16)

    def proj(a, b):
        return jnp.dot(hb, w_ref[:, a:b], preferred_element_type=F32)

    lane = lax.broadcasted_iota(jnp.int32, (tm, LANES), 1)
    ret_first = (lane % (RET_QK_DIM // 2)) < (RET_QK_DIM // 4)
    mla_first = (lane % (MLA_ROPE_DIM // 2)) < (MLA_ROPE_DIM // 4)
    rcos, rsin = rcos_ref[...], rsin_ref[...]

    dq_raw = proj(_O_DQ, _O_DKV)
    dkv_raw = proj(_O_DKV, _O_KR)
    kr_raw = jnp.dot(hb, wkr_ref[...], preferred_element_type=F32)
    q = proj(_O_RQ, _O_RK)
    kt = lax.dot_general(wkt_ref[...], hb, _NT, preferred_element_type=F32) * RET_SCALE
    rv_ref[0] = proj(_O_RV, _O_RG).astype(BF16)
    rg_ref[0] = proj(_O_RG, _O_DQ).astype(BF16)
    dq = (_rms(dq_raw) * qn_ref[...]).astype(BF16)
    dkv = (_rms(dkv_raw) * kvn_ref[...]).astype(BF16)
    qt = lax.dot_general(wuqt_ref[...], dq, _NT, preferred_element_type=F32) * _Q_SCALE
    kn = jnp.dot(dkv, wuk_ref[...], preferred_element_type=F32)
    vt = lax.dot_general(wvt_ref[...], dkv, _NT, preferred_element_type=F32)
    gr_ref[0] = jnp.dot(hb, wg_ref[:, :RET_V_WIDTH], preferred_element_type=F32).astype(BF16)
    gm_ref[0] = jnp.dot(hb, wg_ref[:, RET_V_WIDTH:], preferred_element_type=F32).astype(BF16)

    for g in range(RET_QK_WIDTH // LANES):
        sl = slice(g * LANES, (g + 1) * LANES)
        rq_ref[0, :, sl] = _rope(q[:, sl], rcos, rsin, ret_first, RET_QK_DIM // 4).astype(BF16)

    nr = RET_QK_DIM // 4
    rcos_r, rsin_r = rtab_ref[0:nr, :], rtab_ref[nr:2 * nr, :]
    rcos_c, rsin_c = rtab_ref[2 * nr:3 * nr, :], rtab_ref[3 * nr:4 * nr, :]
    for h in range(RET_HEADS):
        r0 = h * RET_QK_DIM
        a, b = kt[r0:r0 + nr], kt[r0 + nr:r0 + 2 * nr]
        c, d = kt[r0 + 2 * nr:r0 + 3 * nr], kt[r0 + 3 * nr:r0 + 4 * nr]
        head = jnp.concatenate([a * rcos_r - b * rsin_r, b * rcos_r + a * rsin_r,
                                c * rcos_c - d * rsin_c, d * rcos_c + c * rsin_c], axis=0)
        rk_ref[0, r0:r0 + RET_QK_DIM, :] = head.astype(BF16)

    nf = MLA_ROPE_DIM // 4
    cos_r, sin_r = mtab_ref[0:nf, :], mtab_ref[nf:2 * nf, :]
    cos_c, sin_c = mtab_ref[2 * nf:3 * nf, :], mtab_ref[3 * nf:4 * nf, :]
    for h in range(MLA_HEADS):
        r0 = h * LANES + MLA_NOPE_DIM
        a, b = qt[r0:r0 + nf], qt[r0 + nf:r0 + 2 * nf]
        c, d = qt[r0 + 2 * nf:r0 + 3 * nf], qt[r0 + 3 * nf:r0 + 4 * nf]
        head = jnp.concatenate([
            qt[h * LANES:r0],
            a * cos_r - b * sin_r, b * cos_r + a * sin_r,
            c * cos_c - d * sin_c, d * cos_c + c * sin_c,
            qt[r0 + MLA_ROPE_DIM:(h + 1) * LANES]], axis=0)
        qt_ref[0, h * LANES:(h + 1) * LANES, :] = head.astype(BF16)

    kr = _rope(kr_raw, mcos_ref[...], msin_ref[...], mla_first, MLA_ROPE_DIM // 4)
    for h in range(MLA_HEADS):
        sl = slice(h * LANES, (h + 1) * LANES)
        km_ref[0, :, sl] = (kn[:, sl] + kr).astype(BF16)
    pad = _VT_ROWS - MLA_V_DIM
    ones_row = (lax.broadcasted_iota(jnp.int32, (pad, tm), 0) == 0).astype(BF16)
    for h in range(MLA_HEADS):
        vt_ref[0, h * _VT_ROWS:h * _VT_ROWS + MLA_V_DIM, :] = vt[h * MLA_V_DIM:(h + 1) * MLA_V_DIM].astype(BF16)
        vt_ref[0, h * _VT_ROWS + MLA_V_DIM:(h + 1) * _VT_ROWS, :] = ones_row


def _inproj_call(x, mod, k0, w, wkr, wg, q_norm, kv_norm, wuqt, wuk, wvt, wkt, tables):
    bsz, seq, d = x.shape
    tm = min(ROW_TILE, seq)
    row = lambda w_: pl.BlockSpec((1, tm, w_), lambda b, i: (b, i, 0))
    col = lambda r_: pl.BlockSpec((1, r_, tm), lambda b, i: (b, 0, i))
    tab = pl.BlockSpec((tm, LANES), lambda b, i: (i, 0))
    ttab = lambda r_: pl.BlockSpec((r_, tm), lambda b, i: (0, i))
    n_q, n_vt = MLA_HEADS * LANES, MLA_HEADS * _VT_ROWS
    shapes = [(seq, RET_QK_WIDTH), (RET_QK_WIDTH, seq), (seq, RET_V_WIDTH), (seq, RET_V_WIDTH),
              (n_q, seq), (seq, n_q), (n_vt, seq), (seq, RET_V_WIDTH), (seq, RET_V_WIDTH)]
    dtypes = (BF16,) * len(shapes)
    out_specs = [col(s[0]) if i in (1, 4, 6) else row(s[1]) for i, s in enumerate(shapes)]
    return pl.pallas_call(
        _inproj_kernel,
        grid=(bsz, seq // tm),
        in_specs=[
            row(d), _mod_spec(d, k0), _mod_spec(d, k0 + 1),
            _const_spec(w.shape), _const_spec(wkr.shape), _const_spec(wg.shape),
            _const_spec((1, MLA_Q_RANK)), _const_spec((1, MLA_KV_RANK)),
            _const_spec(wuqt.shape), _const_spec(wuk.shape), _const_spec(wvt.shape), _const_spec(wkt.shape),
            tab, tab, tab, tab, ttab(MLA_ROPE_DIM), ttab(RET_QK_DIM),
        ],
        out_specs=out_specs,
        out_shape=[jax.ShapeDtypeStruct((bsz,) + s, dt) for s, dt in zip(shapes, dtypes)],
        compiler_params=_cparams(("parallel", "parallel")),
        name="inproj",
    )(x, mod, mod, w, wkr, wg, q_norm.reshape(1, -1), kv_norm.reshape(1, -1), wuqt, wuk, wvt, wkt, *tables)


def _ret_state_kernel(kf_ref, vf_ref, kb_ref, vb_ref, decf_ref, decb_ref, sf0_ref, sb0_ref,
                      sfs_ref, sbs_ref, sf_ref, sb_ref, dm_ref, qs_ref,
                      st_f, st_b, ks_f, ks_b, cd_f, cd_b, *, group):
    b, s = pl.program_id(0), pl.program_id(1)
    c, dk, dv = RET_CHUNK, RET_QK_DIM, RET_V_DIM

    @pl.when(s == 0)
    def _():
        st_f[...] = sf0_ref[0]
        st_b[...] = sb0_ref[0]
        j = lax.broadcasted_iota(jnp.int32, (1, c), 1).astype(F32)
        for h in range(RET_HEADS):
            lgf = -jnp.exp(decf_ref[h])
            lgb = -jnp.exp(decb_ref[h])
            ks_f[h] = jnp.exp(lgf * (c - 1.0 - j))
            ks_b[h] = jnp.exp(lgb * j)
            cd_f[h] = jnp.broadcast_to(jnp.exp(lgf * c), (1, dv))
            cd_b[h] = jnp.broadcast_to(jnp.exp(lgb * c), (1, dv))

    @pl.when((b == 0) & (s == 0))
    def _():
        ii = lax.broadcasted_iota(jnp.int32, (c, c), 0).astype(F32)
        jj = lax.broadcasted_iota(jnp.int32, (c, c), 1).astype(F32)
        diff = ii - jj
        row = lax.broadcasted_iota(jnp.int32, (c, 2 * dk), 0).astype(F32)
        fwd_half = lax.broadcasted_iota(jnp.int32, (c, 2 * dk), 1) < dk
        for h in range(RET_HEADS):
            lgf = -jnp.exp(decf_ref[h])
            lgb = -jnp.exp(decb_ref[h])
            dm_ref[h] = (jnp.where(diff >= 0, jnp.exp(lgf * jnp.maximum(diff, 0.0)), 0.0)
                         + jnp.where(diff <= 0, jnp.exp(lgb * jnp.maximum(-diff, 0.0)), 0.0))
            qs_ref[h] = jnp.exp(jnp.where(fwd_half, lgf * (row + 1.0), lgb * (c - row)))

    kv = {}
    for g in range(group):
        gb = group - 1 - g
        for h in range(RET_HEADS):
            kf = (kf_ref[0, h * dk:(h + 1) * dk, g * c:(g + 1) * c].astype(F32) * ks_f[h]).astype(BF16)
            kv["f", g, h] = jnp.dot(kf, vf_ref[0, g * c:(g + 1) * c, h * dv:(h + 1) * dv],
                                    preferred_element_type=F32)
            kb = (kb_ref[0, h * dk:(h + 1) * dk, gb * c:(gb + 1) * c].astype(F32) * ks_b[h]).astype(BF16)
            kv["b", g, h] = jnp.dot(kb, vb_ref[0, gb * c:(gb + 1) * c, h * dv:(h + 1) * dv],
                                    preferred_element_type=F32)
    for h in range(RET_HEADS):
        sf, sb = st_f[h], st_b[h]
        for g in range(group):
            gb = group - 1 - g
            sfs_ref[0, g, h * dk:(h + 1) * dk, :] = sf.astype(BF16)
            sf = cd_f[h] * sf + kv["f", g, h]
            sbs_ref[0, gb, h * dk:(h + 1) * dk, :] = sb.astype(BF16)
            sb = cd_b[h] * sb + kv["b", g, h]
        st_f[h], st_b[h] = sf, sb

    @pl.when(s == pl.num_programs(1) - 1)
    def _():
        sf_ref[0] = st_f[...]
        sb_ref[0] = st_b[...]


def _ret_state_call(rkt, rv, dec_f, dec_b, s0_f, s0_b):
    bsz, seq, _ = rv.shape
    c = RET_CHUNK
    nc = seq // c
    group = min(RET_GROUP, nc)
    ns = nc // group
    h, dk, dv = RET_HEADS, RET_QK_DIM, RET_V_DIM
    st_spec = pl.BlockSpec((1, h, dk, dv), lambda b, s: (b, 0, 0, 0))
    dec_spec = pl.BlockSpec((h, 1, 1), lambda b, s: (0, 0, 0))
    tab = lambda w_: pl.BlockSpec((h, c, w_), lambda b, s: (0, 0, 0))
    return pl.pallas_call(
        functools.partial(_ret_state_kernel, group=group),
        grid=(bsz, ns),
        in_specs=[
            pl.BlockSpec((1, h * dk, group * c), lambda b, s: (b, 0, s)),
            pl.BlockSpec((1, group * c, h * dv), lambda b, s: (b, s, 0)),
            pl.BlockSpec((1, h * dk, group * c), lambda b, s: (b, 0, ns - 1 - s)),
            pl.BlockSpec((1, group * c, h * dv), lambda b, s: (b, ns - 1 - s, 0)),
            dec_spec, dec_spec, st_spec, st_spec],
        out_specs=[
            pl.BlockSpec((1, group, h * dk, dv), lambda b, s: (b, s, 0, 0)),
            pl.BlockSpec((1, group, h * dk, dv), lambda b, s: (b, ns - 1 - s, 0, 0)),
            st_spec, st_spec, tab(c), tab(2 * dk)],
        out_shape=[jax.ShapeDtypeStruct((bsz, nc, h * dk, dv), BF16)] * 2
        + [jax.ShapeDtypeStruct((bsz, h, dk, dv), F32)] * 2
        + [jax.ShapeDtypeStruct((h, c, c), F32), jax.ShapeDtypeStruct((h, c, 2 * dk), F32)],
        scratch_shapes=[
            pltpu.VMEM((h, dk, dv), F32), pltpu.VMEM((h, dk, dv), F32),
            pltpu.VMEM((h, 1, c), F32), pltpu.VMEM((h, 1, c), F32),
            pltpu.VMEM((h, 1, dv), F32), pltpu.VMEM((h, 1, dv), F32),
        ],
        compiler_params=_cparams(("arbitrary", "arbitrary")),
        name="ret_state",
    )(rkt, rv, rkt, rv, dec_f.reshape(h, 1, 1), dec_b.reshape(h, 1, 1), s0_f, s0_b)


def _attn_kernel(qt_ref, *refs, tq, tk, depth):
    o_ref = refs[-1]
    sources = [(refs[i], refs[i + 1]) for i in range(0, len(refs) - 1, 2)]
    chunks = [(k_ref, vt_ref, j) for k_ref, vt_ref in sources for j in range(k_ref.shape[1] // tk)]
    nk = len(chunks)
    heads = range(qt_ref.shape[1] // LANES)
    rounds = [(qi, n) for n in range(nk) for qi in range(qt_ref.shape[2] // tq)]

    def scores(hd, qi, n):
        k_ref, _, j = chunks[n]
        k = k_ref[0, j * tk:(j + 1) * tk, hd * LANES:(hd + 1) * LANES]
        qt = qt_ref[0, hd * LANES:(hd + 1) * LANES, qi * tq:(qi + 1) * tq]
        return jnp.dot(k, qt, preferred_element_type=F32)

    pending, m_all, acc_all = {}, {}, {}
    for r in range(-depth, len(rounds)):
        if r + depth < len(rounds):
            for hd in heads:
                pending[hd, r + depth] = scores(hd, *rounds[r + depth])
        if r < 0:
            continue
        qi, n = rounds[r]
        if n == 0:
            m_all[qi] = [jnp.full((1, tq), -jnp.inf, F32) for _ in heads]
            acc_all[qi] = [jnp.zeros((_VT_ROWS, tq), F32) for _ in heads]
        m, acc = m_all[qi], acc_all[qi]
        _, vt_ref, j = chunks[n]
        pts, alphas = [], []
        for hd in heads:
            st = pending.pop((hd, r))
            m_new = jnp.maximum(m[hd], jnp.max(st, axis=0, keepdims=True))
            alphas.append(jnp.exp2(m[hd] - m_new))
            pts.append(jnp.exp2(st - m_new).astype(BF16))
            m[hd] = m_new
        for hd in heads:
            vt = vt_ref[0, hd * _VT_ROWS:(hd + 1) * _VT_ROWS, j * tk:(j + 1) * tk]
            acc[hd] = alphas[hd] * acc[hd] + jnp.dot(vt, pts[hd], preferred_element_type=F32)
        if n == nk - 1:
            outs = [a[:MLA_V_DIM] / a[MLA_V_DIM:MLA_V_DIM + 1] for a in acc]
            o_ref[0, qi * tq:(qi + 1) * tq, :] = jnp.concatenate(outs, axis=0).T.astype(o_ref.dtype)


def _attn_call(qt, *kv):
    bsz, _, seq = qt.shape
    tq = tk = MXU_TILE
    tqs = min(ATTN_SUBTILES * tq, seq)
    hps = ATTN_HEADS_PER_STEP
    in_specs = [pl.BlockSpec((1, hps * LANES, tqs), lambda b, h, i: (b, h, i))]
    for km, vt in zip(kv[::2], kv[1::2]):
        lk = km.shape[1]
        in_specs += [pl.BlockSpec((1, lk, hps * LANES), lambda b, h, i: (b, 0, h)),
                     pl.BlockSpec((1, hps * _VT_ROWS, lk), lambda b, h, i: (b, h, 0))]
    return pl.pallas_call(
        functools.partial(_attn_kernel, tq=tq, tk=tk, depth=2),
        grid=(bsz, MLA_HEADS // hps, seq // tqs),
        in_specs=in_specs,
        out_specs=pl.BlockSpec((1, tqs, hps * MLA_V_DIM), lambda b, h, i: (b, i, h)),
        out_shape=jax.ShapeDtypeStruct((bsz, seq, MLA_OUT_WIDTH), BF16),
        compiler_params=_cparams(("parallel", "parallel", "parallel")),
        name="attention",
    )(qt, *kv)


def _merge_kernel(x_ref, rq_ref, rkt_ref, rv_ref, sfs_ref, sbs_ref, dm_ref, qs_ref,
                  rg_ref, at_ref, gr_ref, gm_ref, gn_ref, g2_ref,
                  wr_ref, wm_ref, wo_ref, o_ref, a_scr):
    c, dk, dv = RET_CHUNK, RET_QK_DIM, RET_V_DIM
    units = [(ci, h) for ci in range(x_ref.shape[1] // c) for h in range(RET_HEADS)]
    lane = lax.broadcasted_iota(jnp.int32, (c, 2 * dk), 1)
    low = lane < dk

    def q_pair(ci, h):
        g = h // 2
        return rq_ref[0, ci * c:(ci + 1) * c, g * 2 * dk:(g + 1) * 2 * dk]

    scores = {}
    for ci, h in units:
        mine = low if h % 2 == 0 else jnp.logical_not(low)
        qm = jnp.where(mine, q_pair(ci, h), jnp.zeros((), BF16))
        g = h // 2
        kt = rkt_ref[0, g * 2 * dk:(g + 1) * 2 * dk, ci * c:(ci + 1) * c]
        scores[ci, h] = jnp.dot(qm, kt, preferred_element_type=F32)
    outs = {}
    for ci, h in units:
        p = (scores.pop((ci, h)) * dm_ref[h]).astype(BF16)
        qf = q_pair(ci, h).astype(F32)
        qr = pltpu.roll(qf, dk, 1)
        both = jnp.where(low, qf, qr) if h % 2 == 0 else jnp.where(low, qr, qf)
        lhs = jnp.concatenate([p, (both * qs_ref[h]).astype(BF16)], axis=1)
        rhs = jnp.concatenate([rv_ref[0, ci * c:(ci + 1) * c, h * dv:(h + 1) * dv],
                               sfs_ref[0, ci, h * dk:(h + 1) * dk, :],
                               sbs_ref[0, ci, h * dk:(h + 1) * dk, :]], axis=0)
        outs[ci, h] = jnp.dot(lhs, rhs, preferred_element_type=F32)
    mla_l = jnp.dot(at_ref[0], wm_ref[...], preferred_element_type=F32)
    for ci, h in units:
        rows, sl = slice(ci * c, (ci + 1) * c), slice(h * dv, (h + 1) * dv)
        o = outs.pop((ci, h))
        d = o - jnp.mean(o, axis=-1, keepdims=True)
        nrm = d * lax.rsqrt(jnp.mean(d * d, axis=-1, keepdims=True) + EPS) * gn_ref[:, sl]
        a_scr[rows, sl] = (_silu(rg_ref[0, rows, sl].astype(F32)) * nrm).astype(BF16)
    ret_l = jnp.dot(a_scr[...], wr_ref[...], preferred_element_type=F32)
    mrg = _sigmoid(gr_ref[0].astype(F32)) * ret_l + _sigmoid(gm_ref[0].astype(F32)) * mla_l
    out = jnp.dot(mrg.astype(BF16), wo_ref[...], preferred_element_type=F32)
    o_ref[0] = x_ref[0] + g2_ref[0] * out


def _merge_call(x, rq, rkt, rv, sfs, sbs, dm, qs, rg, att, gr, gm, gn, mod, k_gate, wr, wm, wo):
    bsz, seq, d = x.shape
    tm = min(ROW_TILE, seq)
    nch = tm // RET_CHUNK
    row = lambda w_: pl.BlockSpec((1, tm, w_), lambda b, i: (b, i, 0))
    state = pl.BlockSpec((1, nch, RET_QK_WIDTH, RET_V_DIM), lambda b, i: (b, i, 0, 0))
    return pl.pallas_call(
        _merge_kernel,
        grid=(bsz, seq // tm),
        in_specs=[row(d), row(RET_QK_WIDTH), pl.BlockSpec((1, RET_QK_WIDTH, tm), lambda b, i: (b, 0, i)),
                  row(RET_V_WIDTH), state, state, _const_spec(dm.shape), _const_spec(qs.shape),
                  row(RET_V_WIDTH), row(MLA_OUT_WIDTH),
                  row(RET_V_WIDTH), row(RET_V_WIDTH), _const_spec((1, RET_V_WIDTH)), _mod_spec(d, k_gate),
                  _const_spec(wr.shape), _const_spec(wm.shape), _const_spec(wo.shape)],
        out_specs=row(d),
        out_shape=jax.ShapeDtypeStruct(x.shape, F32),
        scratch_shapes=[pltpu.VMEM((tm, RET_V_WIDTH), BF16)],
        compiler_params=_cparams(("parallel", "parallel")),
        name="merge",
    )(x, rq, rkt, rv, sfs, sbs, dm, qs, rg, att, gr, gm, gn.reshape(1, -1), mod, wr, wm, wo)


def _rope_tables(length, n_freq):
    f32 = np.float32
    rows = length // GRID_W
    row = np.repeat(np.arange(rows, dtype=f32), GRID_W)
    col = np.tile(np.arange(GRID_W, dtype=f32), rows)
    inv_freq = np.power(f32(ROPE_BASE), -np.arange(n_freq, dtype=f32) / f32(n_freq)).astype(f32)
    ang_r = row[:, None] * inv_freq[None, :]
    ang_c = col[:, None] * inv_freq[None, :]
    return np.cos(ang_r), np.sin(ang_r), np.cos(ang_c), np.sin(ang_c)


def _pair_tables(t):
    cos_r, sin_r, cos_c, sin_c = t
    return (np.concatenate([cos_r, cos_r, cos_c, cos_c], axis=-1),
            np.concatenate([-sin_r, sin_r, -sin_c, sin_c], axis=-1))


def _tables(length, rope):
    f32 = np.float32
    nr, nm = RET_QK_DIM // 4, MLA_ROPE_DIM // 4
    if rope:
        rt, mt = _rope_tables(length, nr), _rope_tables(length, nm)
    else:
        one, zero = np.ones((length, 1), f32), np.zeros((length, 1), f32)
        rt = tuple(np.tile(v, (1, nr)) for v in (one, zero, one, zero))
        mt = tuple(np.tile(v, (1, nm)) for v in (one, zero, one, zero))
    rc, rs = _pair_tables(rt)
    mc, ms = _pair_tables(mt)
    reps = LANES // RET_QK_DIM
    tail = LANES - MLA_NOPE_DIM - MLA_ROPE_DIM
    mcos = np.concatenate([np.ones((length, MLA_NOPE_DIM), f32), mc, np.ones((length, tail), f32)], axis=-1)
    msin = np.concatenate([np.zeros((length, MLA_NOPE_DIM), f32), ms, np.zeros((length, tail), f32)], axis=-1)
    mtab = np.ascontiguousarray(np.concatenate(mt, axis=-1).T)
    rtab = np.ascontiguousarray(np.concatenate(rt, axis=-1).T)
    return np.tile(rc, (1, reps)), np.tile(rs, (1, reps)), mcos, msin, mtab, rtab


def _prep_w_in(w_in, l):
    o_kr = _O_KR
    lead = w_in[l, :, :o_kr].astype(BF16)
    kr = jnp.pad(w_in[l, :, o_kr:o_kr + MLA_ROPE_DIM].astype(BF16),
                 ((0, 0), (MLA_NOPE_DIM, LANES - MLA_NOPE_DIM - MLA_ROPE_DIM)))
    gates = w_in[l, :, o_kr + MLA_ROPE_DIM:].astype(BF16)
    wkt = w_in[l, :, _O_RK:_O_RV].astype(BF16).T
    return lead, kr, gates, wkt


def _prep_w_uq(w_uq):
    r = w_uq.shape[0]
    w = w_uq.reshape(r, MLA_HEADS, MLA_QK_DIM)
    w = jnp.pad(w, ((0, 0), (0, 0), (0, LANES - MLA_QK_DIM)))
    return w.reshape(r, MLA_HEADS * LANES).T.astype(BF16)


def _prep_w_ukv(w_ukv):
    r = w_ukv.shape[0]
    w = w_ukv.reshape(r, MLA_HEADS, MLA_NOPE_DIM + MLA_V_DIM)
    wk = jnp.pad(w[..., :MLA_NOPE_DIM], ((0, 0), (0, 0), (0, LANES - MLA_NOPE_DIM)))
    wv = w[..., MLA_NOPE_DIM:]
    return wk.reshape(r, MLA_HEADS * LANES).astype(BF16), wv.reshape(r, MLA_OUT_WIDTH).T.astype(BF16)


def kernel(x, c, ctx, c_ctx, w_ada, b_ada, ffn1_w1, ffn1_w3, ffn1_w2, ffn2_w1, ffn2_w3, ffn2_w2, w_in,
           ret_decay_fwd, ret_decay_bwd, ret_gn, mla_q_norm, mla_kv_norm, w_uq, w_ukv, w_ret_out,
           w_mla_out, w_o, final_norm):
    bsz, seq, d = x.shape
    lc = ctx.shape[1]
    depth = w_ada.shape[0]

    rows = -(-(bsz + 1) // 8) * 8
    cc = jnp.zeros((rows, d), F32).at[:bsz].set(c).at[bsz].set(c_ctx)
    mod_all = _mod_call(cc, w_ada, b_ada)

    tab_lat = _tables(seq, True)
    tab_ctx = _tables(lc, False)
    zero_state = jnp.zeros((bsz, RET_HEADS, RET_QK_DIM, RET_V_DIM), F32)

    xc = ctx
    for l in range(depth):
        last = l == depth - 1
        mod = mod_all[l, :bsz][:, None, :]
        mod_c = jnp.broadcast_to(mod_all[l, bsz][None, None, :], (bsz, 1, N_MOD * d))
        f1 = (ffn1_w1[l].astype(BF16), ffn1_w3[l].astype(BF16), ffn1_w2[l].astype(BF16))
        f2 = (ffn2_w1[l].astype(BF16), ffn2_w3[l].astype(BF16), ffn2_w2[l].astype(BF16))
        w_lead, w_kr, w_gates, wkt_l = _prep_w_in(w_in, l)
        wuqt_l = _prep_w_uq(w_uq[l])
        wuk_l, wvt_l = _prep_w_ukv(w_ukv[l])
        wr, wm, wo = w_ret_out[l].astype(BF16), w_mla_out[l].astype(BF16), w_o[l].astype(BF16)

        x = _ffn_call(x, mod, 0, *f1)
        xc = _ffn_call(xc, mod_c, 0, *f1)

        mla_w = (w_lead, w_kr, w_gates, mla_q_norm[l], mla_kv_norm[l], wuqt_l, wuk_l, wvt_l, wkt_l)
        rq, rkt, rv, rg, qt, km, vt, gr, gm = _inproj_call(x, mod, 3, *mla_w, tab_lat)
        crq, crkt, crv, crg, cqt, ckm, cvt, cgr, cgm = _inproj_call(xc, mod_c, 3, *mla_w, tab_ctx)

        dec = (ret_decay_fwd[l], ret_decay_bwd[l])
        csf, csb, sc_f, sc_b, dm, qs = _ret_state_call(crkt, crv, *dec, zero_state, zero_state)
        sfs, sbs, _, _, _, _ = _ret_state_call(rkt, rv, *dec, sc_f, sc_b)

        att = _attn_call(qt, km, vt, ckm, cvt)
        x = _merge_call(x, rq, rkt, rv, sfs, sbs, dm, qs, rg, att, gr, gm, ret_gn[l], mod, 5, wr, wm, wo)
        x = _ffn_call(x, mod, 6, *f2, final_gain=final_norm if last else None)
        if not last:
            att_c = _attn_call(cqt, ckm, cvt)
            xc = _merge_call(xc, crq, crkt, crv, csf, csb, dm, qs, crg, att_c, cgr, cgm, ret_gn[l],
                             mod_c, 5, wr, wm, wo)
            xc = _ffn_call(xc, mod_c, 6, *f2)
    return x
```
